```python
import jax, jax.numpy as jnp
from jax import lax
import numpy as np

D_MODEL = 2048
BATCH = 2
SEQ = 16384
DEPTH = 1
DEC_BATCH = 16
DEC_SEQ = 64
PAST_LEN = 1024

CHUNK = 64
D_CONV = 1024
CONV_WIDTH = 3
HGRN_HEADS = 16
HGRN_DK = 128
HGRN_DV = 128
D_HGRN = HGRN_HEADS * HGRN_DK
N_BRANCH = 2
D_FF = 5632
HGRN_BLOCK = 16
NORM_EPS = 1e-6
D_IN = 3 * D_CONV + 4 * D_HGRN + N_BRANCH * D_MODEL

kernel_name = "hybrid_shortconv_hgrn2_macaron_step"


def rms_norm(x, g):
    xf = x.astype(jnp.float32)
    r = lax.rsqrt(jnp.mean(xf * xf, axis=-1, keepdims=True) + NORM_EPS)
    return (xf * r * g.astype(jnp.float32)).astype(x.dtype)


def swiglu(h, w_gate, w_up, w_down):
    return (jax.nn.silu(h @ w_gate) * (h @ w_up)) @ w_down


def causal_conv(u, buf, w):
    t = u.shape[1]
    full = jnp.concatenate([buf.astype(u.dtype), u], axis=1)
    y = w[0] * full[:, 0:t]
    for j in range(1, CONV_WIDTH):
        y = y + w[j] * full[:, j:j + t]
    return y, full[:, -(CONV_WIDTH - 1):]


def hgrn2_recurrence(q, logf, k, v, s0):
    b, t, h, dk = q.shape
    dv = v.shape[-1]
    n = -(-t // HGRN_BLOCK)
    pad = n * HGRN_BLOCK - t
    padw = ((0, 0), (0, pad), (0, 0), (0, 0))

    def blocks(a):
        a = jnp.pad(a, padw)
        return a.reshape(b, n, HGRN_BLOCK, h, a.shape[-1]).transpose(1, 0, 2, 3, 4)

    mask = jnp.tril(jnp.ones((HGRN_BLOCK, HGRN_BLOCK), dtype=bool))

    def step(s, blk):
        qb, gb, kb, vb = blk
        g_cum = jnp.cumsum(gb, axis=1)
        g_last = g_cum[:, -1]
        q_dec = qb * jnp.exp(g_cum)
        k_inv = kb * jnp.exp(-g_cum)
        k_end = kb * jnp.exp(g_last[:, None] - g_cum)
        a = jnp.einsum('blhk,bshk->bhls', q_dec, k_inv)
        a = jnp.where(mask, a, 0.0)
        o = (jnp.einsum('bhls,bshv->blhv', a, vb)
             + jnp.einsum('blhk,bhkv->blhv', q_dec, s))
        s_new = jnp.exp(g_last)[..., None] * s + jnp.einsum('bshk,bshv->bhkv', k_end, vb)
        return s_new, o

    s_fin, o = lax.scan(step, s0.astype(jnp.float32),
                        (blocks(q), blocks(logf), blocks(k), blocks(v)))
    o = o.transpose(1, 0, 2, 3, 4).reshape(b, n * HGRN_BLOCK, h, dv)[:, :t]
    return o, s_fin


def token_mixing(h, lb, conv_buf, s_hgrn, w_in, conv_w, hgrn_norm, w_br_conv, w_br_hgrn, w_out):
    b, t, _ = h.shape
    proj = h @ w_in
    idx = list(np.cumsum([D_CONV, D_CONV, D_CONV, D_HGRN, D_HGRN, D_HGRN, D_HGRN, D_MODEL]))
    gb, gc, vc, q, fz, iv, og, g_conv, g_hgrn = jnp.split(proj, idx, axis=-1)

    conv_y, new_buf = causal_conv(gc * vc, conv_buf, conv_w)
    y_conv = gb * conv_y

    f = lb + (1.0 - lb) * jax.nn.sigmoid(fz.astype(jnp.float32))
    logf = jnp.log(f)
    kk = 1.0 - f
    qf = jax.nn.silu(q.astype(jnp.float32))
    heads = lambda a: a.reshape(b, t, HGRN_HEADS, -1)
    o, new_s = hgrn2_recurrence(heads(qf), heads(logf), heads(kk),
                                heads(iv.astype(jnp.float32)), s_hgrn)
    o = o * lax.rsqrt(jnp.mean(o * o, axis=-1, keepdims=True) + NORM_EPS)
    o = o * hgrn_norm.astype(jnp.float32).reshape(HGRN_HEADS, HGRN_DV)
    y_hgrn = (o.reshape(b, t, D_HGRN) * jax.nn.silu(og.astype(jnp.float32))).astype(h.dtype)

    merged = (jax.nn.sigmoid(g_conv) * (y_conv @ w_br_conv)
              + jax.nn.sigmoid(g_hgrn) * (y_hgrn @ w_br_hgrn))
    return merged @ w_out, new_buf, new_s


def trunk(x, conv_bufs, hgrn_states, w):
    lb_all = jnp.cumsum(jax.nn.softmax(w['hgrn_lb_logits'].astype(jnp.float32), axis=0), axis=0)
    new_convs, new_states = [], []
    for l in range(DEPTH):
        hn = rms_norm(x, w['norm_ffn1'][l])
        x = x + 0.5 * swiglu(hn, w['w_ffn1_gate'][l], w['w_ffn1_up'][l], w['w_ffn1_down'][l])
        hn = rms_norm(x, w['norm_mix'][l])
        m, nb, ns = token_mixing(hn, lb_all[l], conv_bufs[l], hgrn_states[l], w['w_in'][l],
                                 w['conv_w'][l], w['hgrn_norm'][l], w['w_br_conv'][l],
                                 w['w_br_hgrn'][l], w['w_out'][l])
        x = x + m
        hn = rms_norm(x, w['norm_ffn2'][l])
        x = x + 0.5 * swiglu(hn, w['w_ffn2_gate'][l], w['w_ffn2_up'][l], w['w_ffn2_down'][l])
        new_convs.append(nb)
        new_states.append(ns)
    y = rms_norm(x, w['norm_final'])
    return y, jnp.stack(new_convs), jnp.stack(new_states)


def setup_inputs(seed: int = 0) -> dict:
    key = jax.random.key(seed)
    ks = jax.random.split(key, 24)
    nrm = lambda k, shape, s: jax.random.normal(k, shape, jnp.float32) * s
    gain = lambda k, shape: 1.0 + 0.01 * jax.random.normal(k, shape, jnp.float32)
    L = DEPTH
    return {
        "x_prompt": nrm(ks[0], (BATCH, SEQ, D_MODEL), 1.0),
        "x_sample": nrm(ks[1], (DEC_BATCH, DEC_SEQ, D_MODEL), 1.0),
        "cache_conv": nrm(ks[2], (L, DEC_BATCH, CONV_WIDTH - 1, D_CONV), 1.0),
        "state_hgrn": nrm(ks[3], (L, DEC_BATCH, HGRN_HEADS, HGRN_DK, HGRN_DV), 0.5),
        "norm_ffn1": gain(ks[4], (L, D_MODEL)),
        "w_ffn1_gate": nrm(ks[5], (L, D_MODEL, D_FF), D_MODEL ** -0.5),
        "w_ffn1_up": nrm(ks[6], (L, D_MODEL, D_FF), D_MODEL ** -0.5),
        "w_ffn1_down": nrm(ks[7], (L, D_FF, D_MODEL), D_FF ** -0.5),
        "norm_mix": gain(ks[8], (L, D_MODEL)),
        "w_in": nrm(ks[9], (L, D_MODEL, D_IN), D_MODEL ** -0.5),
        "conv_w": nrm(ks[10], (L, CONV_WIDTH, D_CONV), CONV_WIDTH ** -0.5),
        "hgrn_lb_logits": nrm(ks[11], (L + 1, D_HGRN), 0.1),
        "hgrn_norm": gain(ks[12], (L, D_HGRN)),
        "w_br_conv": nrm(ks[13], (L, D_CONV, D_MODEL), D_CONV ** -0.5),
        "w_br_hgrn": nrm(ks[14], (L, D_HGRN, D_MODEL), D_HGRN ** -0.5),
        "w_out": nrm(ks[15], (L, D_MODEL, D_MODEL), D_MODEL ** -0.5),
        "norm_ffn2": gain(ks[16], (L, D_MODEL)),
        "w_ffn2_gate": nrm(ks[17], (L, D_MODEL, D_FF), D_MODEL ** -0.5),
        "w_ffn2_up": nrm(ks[18], (L, D_MODEL, D_FF), D_MODEL ** -0.5),
        "w_ffn2_down": nrm(ks[19], (L, D_FF, D_MODEL), D_FF ** -0.5),
        "norm_final": gain(ks[20], (D_MODEL,)),
    }


def reference(x_prompt, x_sample, cache_conv, state_hgrn, norm_ffn1, w_ffn1_gate, w_ffn1_up,
              w_ffn1_down, norm_mix, w_in, conv_w, hgrn_lb_logits, hgrn_norm, w_br_conv,
              w_br_hgrn, w_out, norm_ffn2, w_ffn2_gate, w_ffn2_up, w_ffn2_down, norm_final):
    w = dict(norm_ffn1=norm_ffn1, w_ffn1_gate=w_ffn1_gate, w_ffn1_up=w_ffn1_up,
             w_ffn1_down=w_ffn1_down, norm_mix=norm_mix, w_in=w_in, conv_w=conv_w,
             hgrn_lb_logits=hgrn_lb_logits, hgrn_norm=hgrn_norm, w_br_conv=w_br_conv,
             w_br_hgrn=w_br_hgrn, w_out=w_out, norm_ffn2=norm_ffn2, w_ffn2_gate=w_ffn2_gate,
             w_ffn2_up=w_ffn2_up, w_ffn2_down=w_ffn2_down, norm_final=norm_final)
    zero_conv = jnp.zeros((DEPTH, x_prompt.shape[0], CONV_WIDTH - 1, D_CONV), x_prompt.dtype)
    zero_hgrn = jnp.zeros((DEPTH, x_prompt.shape[0], HGRN_HEADS, HGRN_DK, HGRN_DV), jnp.float32)
    y_prompt, new_conv_prompt, new_hgrn_prompt = trunk(x_prompt, zero_conv, zero_hgrn, w)
    y_sample, new_conv_sample, new_hgrn_sample = trunk(x_sample, cache_conv, state_hgrn, w)
    return (y_prompt, y_sample, new_conv_prompt, new_hgrn_prompt, new_conv_sample, new_hgrn_sample)
```

```python
import functools

import jax
import jax.numpy as jnp
from jax import lax
from jax.experimental import pallas as pl
from jax.experimental.pallas import tpu as pltpu

NORM_EPS = 1e-6
CONV_WIDTH = 3
HGRN_BLOCK = 16
HEAD_DIM = 128
HEADS_PER_STEP = 2
VMEM_LIMIT_BYTES = 56 * 1024 * 1024

F32 = jnp.float32
BF16 = jnp.bfloat16


def _params(semantics):
    return pltpu.CompilerParams(dimension_semantics=semantics, vmem_limit_bytes=VMEM_LIMIT_BYTES)


def _rms(x, gain):
    r = lax.rsqrt(jnp.mean(x * x, axis=-1, keepdims=True) + NORM_EPS)
    return x * r * gain


def _dot(a, b):
    return jnp.dot(a, b, preferred_element_type=F32)


def _dot_nt(a, b):
    return lax.dot_general(a, b, (((1,), (1,)), ((), ())), preferred_element_type=F32)


def _dot_tn(a, b):
    return lax.dot_general(a, b, (((0,), (0,)), ((), ())), preferred_element_type=F32)


def _ffn_kernel(x_ref, g_ref, wg_ref, wu_ref, wd_ref, gf_ref, o_ref, h_ref, *, final_norm):
    f = pl.program_id(1)

    @pl.when(f == 0)
    def _():
        x = x_ref[...]
        h_ref[...] = _rms(x, g_ref[...]).astype(BF16)
        o_ref[...] = x

    h = h_ref[...]
    g = _dot(h, wg_ref[...])
    u = _dot(h, wu_ref[...])
    a = (g * jax.nn.sigmoid(g) * u * 0.5).astype(BF16)
    o_ref[...] += _dot(a, wd_ref[...])

    if final_norm:
        @pl.when(f == pl.num_programs(1) - 1)
        def _():
            o_ref[...] = _rms(o_ref[...], gf_ref[...])


def _ffn(x, gain, wg, wu, wd, final_gain, *, final_norm, tm, tf):
    m, d = x.shape
    dff = wg.shape[1]
    grid = (m // tm, dff // tf)
    return pl.pallas_call(
        functools.partial(_ffn_kernel, final_norm=final_norm),
        grid=grid,
        in_specs=[
            pl.BlockSpec((tm, d), lambda i, f: (i, 0)),
            pl.BlockSpec((1, d), lambda i, f: (0, 0)),
            pl.BlockSpec((d, tf), lambda i, f: (0, f)),
            pl.BlockSpec((d, tf), lambda i, f: (0, f)),
            pl.BlockSpec((tf, d), lambda i, f: (f, 0)),
            pl.BlockSpec((1, d), lambda i, f: (0, 0)),
        ],
        out_specs=pl.BlockSpec((tm, d), lambda i, f: (i, 0)),
        out_shape=jax.ShapeDtypeStruct((m, d), F32),
        scratch_shapes=[pltpu.VMEM((tm, d), BF16)],
        compiler_params=_params(("arbitrary", "arbitrary")),
        name="ffn_final" if final_norm else "ffn",
    )(x, gain, wg, wu, wd, final_gain)


def _proj_kernel(x_ref, g_ref, w_ref, o_ref, h_ref, *, group_width):
    @pl.when(pl.program_id(1) == 0)
    def _():
        h_ref[...] = _rms(x_ref[...], g_ref[...]).astype(BF16)

    res = _dot(h_ref[...], w_ref[...]).astype(o_ref.dtype)
    for k in range(o_ref.shape[0]):
        o_ref[k] = res[:, k * group_width:(k + 1) * group_width]


def _proj(x, gain, w, *, group_width, tm, tn):
    m, d = x.shape
    n = w.shape[1]
    gpb = tn // group_width
    return pl.pallas_call(
        functools.partial(_proj_kernel, group_width=group_width),
        grid=(m // tm, n // tn),
        in_specs=[
            pl.BlockSpec((tm, d), lambda i, j: (i, 0)),
            pl.BlockSpec((1, d), lambda i, j: (0, 0)),
            pl.BlockSpec((d, tn), lambda i, j: (0, j)),
        ],
        out_specs=pl.BlockSpec((gpb, tm, group_width), lambda i, j: (j, i, 0)),
        out_shape=jax.ShapeDtypeStruct((n // group_width, m, group_width), BF16),
        scratch_shapes=[pltpu.VMEM((tm, d), BF16)],
        compiler_params=_params(("arbitrary", "arbitrary")),
        name="in_proj",
    )(x, gain, w)


def _conv_kernel(b_ref, c_ref, v_ref, c0_ref, w_ref, y_ref, tail_ref, ubuf, *, tc):
    @pl.when(pl.program_id(1) == 0)
    def _():
        tail_ref[...] = c0_ref[...]

    pad = 8
    u = c_ref[0].astype(F32) * v_ref[0].astype(F32)
    ubuf[pad - (CONV_WIDTH - 1):pad, :] = tail_ref[0]
    ubuf[pad:pad + tc, :] = u
    w = w_ref[...]
    acc = w[CONV_WIDTH - 1:CONV_WIDTH] * u
    for j in range(CONV_WIDTH - 1):
        shift = CONV_WIDTH - 1 - j
        acc = acc + w[j:j + 1] * ubuf[pad - shift:pad - shift + tc, :]
    y_ref[...] = (b_ref[0].astype(F32) * acc).astype(y_ref.dtype)
    tail_ref[0] = u[tc - (CONV_WIDTH - 1):, :]


def _conv(pc, conv0, conv_w, *, batch, tc):
    _, m, dc = pc.shape
    t = m // batch
    nc = t // tc
    row = lambda b, c: b * nc + c
    return pl.pallas_call(
        functools.partial(_conv_kernel, tc=tc),
        grid=(batch, nc),
        in_specs=[
            pl.BlockSpec((1, tc, dc), lambda b, c: (0, row(b, c), 0)),
            pl.BlockSpec((1, tc, dc), lambda b, c: (1, row(b, c), 0)),
            pl.BlockSpec((1, tc, dc), lambda b, c: (2, row(b, c), 0)),
            pl.BlockSpec((1, CONV_WIDTH - 1, dc), lambda b, c: (b, 0, 0)),
            pl.BlockSpec((CONV_WIDTH, dc), lambda b, c: (0, 0)),
        ],
        out_specs=[
            pl.BlockSpec((tc, dc), lambda b, c: (row(b, c), 0)),
            pl.BlockSpec((1, CONV_WIDTH - 1, dc), lambda b, c: (b, 0, 0)),
        ],
        out_shape=[
            jax.ShapeDtypeStruct((m, dc), BF16),
            jax.ShapeDtypeStruct((batch, CONV_WIDTH - 1, dc), F32),
        ],
        scratch_shapes=[pltpu.VMEM((tc + 8, dc), F32)],
        compiler_params=_params(("arbitrary", "arbitrary")),
        name="short_conv",
    )(pc, pc, pc, conv0, conv_w)


def _ref_rows(c, rows, half):
    g = c.shape[0]
    out = None
    for s in reversed(range(g // (2 * half))):
        mid = s * 2 * half + half - 1
        val = c[mid:mid + 1]
        out = val if out is None else jnp.where(rows < (s + 1) * 2 * half, val, out)
    return out


def _hgrn_kernel(q_ref, f_ref, i_ref, og_ref, lbl_ref, nrm_ref, s0_ref, y_ref, s_ref, st_ref,
                 *, layer, group, n_groups):
    chunk = pl.program_id(2)
    width = HEADS_PER_STEP * HEAD_DIM

    @pl.when(chunk == 0)
    def _():
        for hh in range(HEADS_PER_STEP):
            st_ref[hh] = s0_ref[0, hh].T

    logits = lbl_ref[...]
    e = jnp.exp(logits - jnp.max(logits, axis=0, keepdims=True))
    lb = jnp.sum(e[:layer + 1], axis=0, keepdims=True) / jnp.sum(e, axis=0, keepdims=True)
    gain = nrm_ref[...]

    rows = lax.broadcasted_iota(jnp.int32, (group, width), 0)
    r2 = lax.broadcasted_iota(jnp.int32, (group, group), 0)
    c2 = lax.broadcasted_iota(jnp.int32, (group, group), 1)
    tri = (c2 <= r2).astype(BF16)
    halves = []
    h = HGRN_BLOCK
    while h < group:
        halves.append(h)
        h *= 2
    mask_diag = (r2 // HGRN_BLOCK == c2 // HGRN_BLOCK) & (c2 <= r2)
    masks = [(r2 // (2 * h) == c2 // (2 * h)) & ((r2 // h) % 2 == 1) & ((c2 // h) % 2 == 0) for h in halves]

    def body(gi, carry):
        r0 = pl.multiple_of(gi * group, group)
        q = q_ref[0, pl.ds(r0, group), :].astype(F32)
        fz = f_ref[0, pl.ds(r0, group), :].astype(F32)
        v = i_ref[0, pl.ds(r0, group), :]
        og = og_ref[0, pl.ds(r0, group), :].astype(F32)

        f = lb + (1.0 - lb) * jax.nn.sigmoid(fz)
        logf = jnp.log(f)
        k = 1.0 - f
        qf = q * jax.nn.sigmoid(q)

        p0 = logf.astype(BF16)
        rem = logf - p0.astype(F32)
        p1 = rem.astype(BF16)
        p2 = (rem - p1.astype(F32)).astype(BF16)
        c = _dot(tri, p0) + _dot(tri, p1) + _dot(tri, p2)
        c_last = c[group - 1:group]

        cb = jnp.zeros_like(c[0:1])
        for blk in range(1, group // HGRN_BLOCK):
            cb = jnp.where(rows < blk * HGRN_BLOCK, cb, c[blk * HGRN_BLOCK - 1:blk * HGRN_BLOCK])
        e_loc = c - cb
        qd = (qf * jnp.exp(e_loc)).astype(BF16)
        kd = (k * jnp.exp(-e_loc)).astype(BF16)
        q_lv, k_lv = [], []
        for h in halves:
            d = c - _ref_rows(c, rows, h)
            q_lv.append(qd if h == HGRN_BLOCK else (qf * jnp.exp(jnp.minimum(d, 0.0))).astype(BF16))
            k_lv.append((k * jnp.exp(jnp.minimum(-d, 0.0))).astype(BF16))
        q_in = (qf * jnp.exp(c)).astype(BF16)
        k_out = (k * jnp.exp(c_last - c)).astype(BF16)
        decay = jnp.exp(c_last)

        outs = []
        for hh in range(HEADS_PER_STEP):
            sl = slice(hh * HEAD_DIM, (hh + 1) * HEAD_DIM)
            a = jnp.where(mask_diag, _dot_nt(qd[:, sl], kd[:, sl]), 0.0)
            for ql, kl, mk in zip(q_lv, k_lv, masks):
                a = jnp.where(mk, _dot_nt(ql[:, sl], kl[:, sl]), a)
            st = st_ref[hh]
            o = _dot(a.astype(BF16), v[:, sl]) + _dot_nt(q_in[:, sl], st.astype(BF16))
            st_ref[hh] = st * decay[:, sl] + _dot_tn(v[:, sl], k_out[:, sl])
            o = o * lax.rsqrt(jnp.mean(o * o, axis=-1, keepdims=True) + NORM_EPS)
            outs.append(o)
        o = jnp.concatenate(outs, axis=1) * gain
        y_ref[0, pl.ds(r0, group), :] = (o * (og * jax.nn.sigmoid(og))).astype(y_ref.dtype)
        return carry

    lax.fori_loop(0, n_groups, body, 0)

    @pl.when(chunk == pl.num_programs(2) - 1)
    def _():
        for hh in range(HEADS_PER_STEP):
            s_ref[0, hh] = st_ref[hh].T


def _hgrn(ph, lb_logits, hgrn_norm, s0, *, layer, batch, tc, group):
    n4, m, width = ph.shape
    npairs = n4 // 4
    t = m // batch
    nc = t // tc
    row = lambda b, c: b * nc + c
    part = lambda p: pl.BlockSpec((1, tc, width), lambda b, hp, c: (p * npairs + hp, row(b, c), 0))
    state_spec = pl.BlockSpec((1, HEADS_PER_STEP, HEAD_DIM, HEAD_DIM), lambda b, hp, c: (b, hp, 0, 0))
    return pl.pallas_call(
        functools.partial(_hgrn_kernel, layer=layer, group=group, n_groups=tc // group),
        grid=(batch, npairs, nc),
        in_specs=[
            part(0), part(1), part(2), part(3),
            pl.BlockSpec((lb_logits.shape[0], width), lambda b, hp, c: (0, hp)),
            pl.BlockSpec((1, width), lambda b, hp, c: (0, hp)),
            state_spec,
        ],
        out_specs=[
            pl.BlockSpec((1, tc, width), lambda b, hp, c: (hp, row(b, c), 0)),
            state_spec,
        ],
        out_shape=[
            jax.ShapeDtypeStruct((npairs, m, width), BF16),
            jax.ShapeDtypeStruct(s0.shape, F32),
        ],
        scratch_shapes=[pltpu.VMEM((HEADS_PER_STEP, HEAD_DIM, HEAD_DIM), F32)],
        compiler_params=_params(("arbitrary", "arbitrary", "arbitrary")),
        name="hgrn2",
    )(ph, ph, ph, ph, lb_logits, hgrn_norm, s0)


def _merge_kernel(yc_ref, yh_ref, pg_ref, x_ref, wbc_ref, wbh_ref, wo_ref, o_ref):
    d = x_ref.shape[1]
    bc = _dot(yc_ref[...], wbc_ref[...])
    bh = _dot(yh_ref[0], wbh_ref[0])
    for p in range(1, yh_ref.shape[0]):
        bh = bh + _dot(yh_ref[p], wbh_ref[p])
    gc = jax.nn.sigmoid(pg_ref[0].astype(F32))
    gh = jax.nn.sigmoid(pg_ref[1].astype(F32))
    merged = (gc * bc + gh * bh).astype(BF16)
    o_ref[...] = x_ref[...] + _dot(merged, wo_ref[...])
    del d


def _merge(yc, yh, pg, x, wbc, wbh, wo, *, tm):
    m, d = x.shape
    npairs, _, width = yh.shape
    dc = yc.shape[1]
    const = lambda shape: pl.BlockSpec(shape, lambda i: (0,) * len(shape), pipeline_mode=pl.Buffered(1))
    return pl.pallas_call(
        _merge_kernel,
        grid=(m // tm,),
        in_specs=[
            pl.BlockSpec((tm, dc), lambda i: (i, 0)),
            pl.BlockSpec((npairs, tm, width), lambda i: (0, i, 0)),
            pl.BlockSpec((2, tm, d), lambda i: (0, i, 0)),
            pl.BlockSpec((tm, d), lambda i: (i, 0)),
            const(wbc.shape), const(wbh.shape), const(wo.shape),
        ],
        out_specs=pl.BlockSpec((tm, d), lambda i: (i, 0)),
        out_shape=jax.ShapeDtypeStruct((m, d), F32),
        compiler_params=_params(("arbitrary",)),
        name="merge_out",
    )(yc, yh, pg, x, wbc, wbh, wo)


def _trunk(x, conv0, s0, w, *, tm_ffn, tf, tm_proj, tm_merge, tc_conv, tc_hgrn, group):
    batch, t, d = x.shape
    m = batch * t
    xf = x.reshape(m, d)
    pair_w = HEADS_PER_STEP * HEAD_DIM

    x1 = _ffn(xf, w["norm_ffn1"], w["ffn1_gate"], w["ffn1_up"], w["ffn1_down"], w["norm_final"],
              final_norm=False, tm=tm_ffn, tf=tf)
    dc = w["in_conv"].shape[1] // 3
    pc = _proj(x1, w["norm_mix"], w["in_conv"], group_width=dc, tm=tm_proj, tn=dc)
    ph = _proj(x1, w["norm_mix"], w["in_hgrn"], group_width=pair_w, tm=tm_proj, tn=1024)
    pg = _proj(x1, w["norm_mix"], w["in_gate"], group_width=d, tm=tm_proj, tn=d)

    yc, conv_new = _conv(pc, conv0, w["conv_w"], batch=batch, tc=tc_conv)
    yh, s_new = _hgrn(ph, w["lb_logits"], w["hgrn_norm"], s0, layer=0, batch=batch, tc=tc_hgrn, group=group)

    x2 = _merge(yc, yh, pg, x1, w["br_conv"], w["br_hgrn"], w["out"], tm=tm_merge)
    y = _ffn(x2, w["norm_ffn2"], w["ffn2_gate"], w["ffn2_up"], w["ffn2_down"], w["norm_final"],
             final_norm=True, tm=tm_ffn, tf=tf)
    return y.reshape(batch, t, d), conv_new, s_new


def kernel(x_prompt, x_sample, cache_conv, state_hgrn, norm_ffn1, w_ffn1_gate, w_ffn1_up, w_ffn1_down, norm_mix, w_in, conv_w, hgrn_lb_logits, hgrn_norm, w_br_conv, w_br_hgrn, w_out, norm_ffn2, w_ffn2_gate, w_ffn2_up, w_ffn2_down, norm_final):
    depth = w_in.shape[0]
    assert depth == 1, "single-layer trunk"
    d = x_prompt.shape[-1]
    dc = cache_conv.shape[-1]
    heads, dk, dv = state_hgrn.shape[-3:]
    assert dk == HEAD_DIM and dv == HEAD_DIM and heads % HEADS_PER_STEP == 0
    dh = heads * dk
    pair_w = HEADS_PER_STEP * HEAD_DIM
    bf = lambda a: a.astype(BF16)
    win = w_in[0]
    w = dict(
        norm_ffn1=norm_ffn1, ffn1_gate=bf(w_ffn1_gate[0]), ffn1_up=bf(w_ffn1_up[0]), ffn1_down=bf(w_ffn1_down[0]),
        norm_mix=norm_mix,
        in_conv=bf(win[:, :3 * dc]), in_hgrn=bf(win[:, 3 * dc:3 * dc + 4 * dh]), in_gate=bf(win[:, 3 * dc + 4 * dh:]),
        conv_w=conv_w[0], lb_logits=hgrn_lb_logits, hgrn_norm=hgrn_norm,
        br_conv=bf(w_br_conv[0]), br_hgrn=bf(w_br_hgrn[0]).reshape(dh // pair_w, pair_w, d), out=bf(w_out[0]),
        norm_ffn2=norm_ffn2, ffn2_gate=bf(w_ffn2_gate[0]), ffn2_up=bf(w_ffn2_up[0]), ffn2_down=bf(w_ffn2_down[0]),
        norm_final=norm_final.reshape(1, d),
    )
    bp = x_prompt.shape[0]
    zero_conv = jnp.zeros((bp,) + cache_conv.shape[2:], F32)
    zero_hgrn = jnp.zeros((bp,) + state_hgrn.shape[2:], F32)
    yp, cp, sp = _trunk(x_prompt, zero_conv, zero_hgrn, w, tm_ffn=512, tf=512, tm_proj=1024, tm_merge=256,
                        tc_conv=512, tc_hgrn=1024, group=128)
    ys, cs, ss = _trunk(x_sample, cache_conv[0], state_hgrn[0], w, tm_ffn=512, tf=512, tm_proj=1024, tm_merge=256,
                        tc_conv=64, tc_hgrn=64, group=64)
    return yp, ys, cp[None], sp[None], cs[None], ss[None]
```

```python
import functools

import numpy as np
import jax
import jax.numpy as jnp
from jax import lax
from jax.experimental import pallas as pl
from jax.experimental.pallas import tpu as pltpu

NORM_EPS = 1e-6
INV_LN2 = 1.4426950408889634
CONV_WIDTH = 3
HGRN_BLOCK = 16
HEAD_DIM = 128
HEADS_PER_STEP = 2
LANE_GROUP = HEADS_PER_STEP * HEAD_DIM
VMEM_LIMIT_BYTES = 56 * 1024 * 1024

F32 = jnp.float32
BF16 = jnp.bfloat16


def _params(semantics):
    return pltpu.CompilerParams(dimension_semantics=semantics, vmem_limit_bytes=VMEM_LIMIT_BYTES)


def _rms(x, gain):
    r = lax.rsqrt(jnp.mean(x * x, axis=-1, keepdims=True) + NORM_EPS)
    return x * r * gain


def _dot(a, b):
    return jnp.dot(a, b, preferred_element_type=F32)


def _dot_nt(a, b):
    return lax.dot_general(a, b, (((1,), (1,)), ((), ())), preferred_element_type=F32)


def _dot_tn(a, b):
    return lax.dot_general(a, b, (((0,), (0,)), ((), ())), preferred_element_type=F32)


def _ffn_kernel(x_ref, g_ref, wg_ref, wu_ref, wd_ref, gf_ref, o_ref, h_ref, *, final_norm):
    f = pl.program_id(1)

    @pl.when(f == 0)
    def _():
        x = x_ref[...]
        h_ref[...] = _rms(x, g_ref[...]).astype(BF16)
        o_ref[...] = x

    h = h_ref[...]
    g = _dot(h, wg_ref[...])
    u = _dot(h, wu_ref[...])
    a = (g * jax.nn.sigmoid(g) * u * 0.5).astype(BF16)
    o_ref[...] += _dot(a, wd_ref[...])

    if final_norm:
        @pl.when(f == pl.num_programs(1) - 1)
        def _():
            o_ref[...] = _rms(o_ref[...], gf_ref[...])


def _ffn(x, gain, wg, wu, wd, final_gain, *, final_norm, tm, tf):
    m, d = x.shape
    dff = wg.shape[1]
    grid = (m // tm, dff // tf)
    return pl.pallas_call(
        functools.partial(_ffn_kernel, final_norm=final_norm),
        grid=grid,
        in_specs=[
            pl.BlockSpec((tm, d), lambda i, f: (i, 0)),
            pl.BlockSpec((1, d), lambda i, f: (0, 0)),
            pl.BlockSpec((d, tf), lambda i, f: (0, f)),
            pl.BlockSpec((d, tf), lambda i, f: (0, f)),
            pl.BlockSpec((tf, d), lambda i, f: (f, 0)),
            pl.BlockSpec((1, d), lambda i, f: (0, 0)),
        ],
        out_specs=pl.BlockSpec((tm, d), lambda i, f: (i, 0)),
        out_shape=jax.ShapeDtypeStruct((m, d), F32),
        scratch_shapes=[pltpu.VMEM((tm, d), BF16)],
        compiler_params=_params(("arbitrary", "arbitrary")),
        name="ffn_final" if final_norm else "ffn",
    )(x, gain, wg, wu, wd, final_gain)


def _proj_kernel(x_ref, g_ref, w_ref, o_ref, h_ref, *, group_width):
    @pl.when(pl.program_id(1) == 0)
    def _():
        h_ref[...] = _rms(x_ref[...], g_ref[...]).astype(BF16)

    res = _dot(h_ref[...], w_ref[...]).astype(o_ref.dtype)
    for k in range(o_ref.shape[0]):
        o_ref[k] = res[:, k * group_width:(k + 1) * group_width]


def _proj(x, gain, w, *, group_width, tm, tn):
    m, d = x.shape
    n = w.shape[1]
    gpb = tn // group_width
    return pl.pallas_call(
        functools.partial(_proj_kernel, group_width=group_width),
        grid=(m // tm, n // tn),
        in_specs=[
            pl.BlockSpec((tm, d), lambda i, j: (i, 0)),
            pl.BlockSpec((1, d), lambda i, j: (0, 0)),
            pl.BlockSpec((d, tn), lambda i, j: (0, j)),
        ],
        out_specs=pl.BlockSpec((gpb, tm, group_width), lambda i, j: (j, i, 0)),
        out_shape=jax.ShapeDtypeStruct((n // group_width, m, group_width), BF16),
        scratch_shapes=[pltpu.VMEM((tm, d), BF16)],
        compiler_params=_params(("arbitrary", "arbitrary")),
        name="in_proj",
    )(x, gain, w)


def _conv_kernel(b_ref, c_ref, v_ref, c0_ref, w_ref, y_ref, tail_ref, ubuf, *, tc):
    @pl.when(pl.program_id(1) == 0)
    def _():
        tail_ref[...] = c0_ref[...]

    ng = y_ref.shape[0]
    wide = lambda ref: jnp.concatenate([ref[g] for g in range(ng)], axis=1).astype(F32)
    pad = 8
    u = wide(c_ref) * wide(v_ref)
    ubuf[pad - (CONV_WIDTH - 1):pad, :] = tail_ref[0]
    ubuf[pad:pad + tc, :] = u
    w = w_ref[...]
    acc = w[CONV_WIDTH - 1:CONV_WIDTH] * u
    for j in range(CONV_WIDTH - 1):
        shift = CONV_WIDTH - 1 - j
        acc = acc + w[j:j + 1] * ubuf[pad - shift:pad - shift + tc, :]
    y = (wide(b_ref) * acc).astype(y_ref.dtype)
    for g in range(ng):
        y_ref[g] = y[:, g * LANE_GROUP:(g + 1) * LANE_GROUP]
    tail_ref[0] = u[tc - (CONV_WIDTH - 1):, :]


def _conv(p, conv0, conv_w, *, first_group, batch, tc):
    _, m, gw = p.shape
    dc = conv0.shape[-1]
    ng = dc // gw
    assert first_group % ng == 0
    t = m // batch
    nc = t // tc
    row = lambda b, c: b * nc + c
    part = lambda k: pl.BlockSpec((ng, tc, gw), lambda b, c: (first_group // ng + k, row(b, c), 0))
    return pl.pallas_call(
        functools.partial(_conv_kernel, tc=tc),
        grid=(batch, nc),
        in_specs=[
            part(0), part(1), part(2),
            pl.BlockSpec((1, CONV_WIDTH - 1, dc), lambda b, c: (b, 0, 0)),
            pl.BlockSpec((CONV_WIDTH, dc), lambda b, c: (0, 0)),
        ],
        out_specs=[
            pl.BlockSpec((ng, tc, gw), lambda b, c: (0, row(b, c), 0)),
            pl.BlockSpec((1, CONV_WIDTH - 1, dc), lambda b, c: (b, 0, 0)),
        ],
        out_shape=[
            jax.ShapeDtypeStruct((ng, m, gw), BF16),
            jax.ShapeDtypeStruct((batch, CONV_WIDTH - 1, dc), F32),
        ],
        scratch_shapes=[pltpu.VMEM((tc + 8, dc), F32)],
        compiler_params=_params(("arbitrary", "arbitrary")),
        name="short_conv",
    )(p, p, p, conv0, conv_w)


def _hgrn_tables(group, width):
    nb = group // HGRN_BLOCK
    halves = []
    h = HGRN_BLOCK
    while h < group:
        halves.append(h)
        h *= 2
    assert halves, "group must span at least two blocks"
    tok = np.arange(group)

    def span(b0, b1):
        return ((tok >= b0 * HGRN_BLOCK) & (tok < b1 * HGRN_BLOCK)).astype(np.float32)

    zero = np.zeros(group, np.float32)
    rows, valid, index = [], [], {}
    for h in halves:
        hb = h // HGRN_BLOCK
        krows, kval, qrows, qval = [], [], [], []
        for b in range(nb):
            bmid = (b // (2 * hb)) * 2 * hb + hb
            first = b < bmid
            krows.append(span(b, bmid) if first else zero)
            kval.append(1.0 if first else 0.0)
            qrows.append(zero if first else span(bmid, b))
            qval.append(0.0 if first else 1.0)
        index[("k", h)] = len(rows)
        rows += krows
        valid += kval
        if hb > 1:
            index[("q", h)] = len(rows)
            rows += qrows
            valid += qval
    index["q_in"] = len(rows)
    rows += [span(0, b) for b in range(nb)]
    valid += [1.0] * nb
    index["k_out"] = len(rows)
    rows += [span(b, nb) for b in range(nb)]
    valid += [1.0] * nb
    index["decay"] = len(rows)
    rows.append(span(0, nb))
    valid.append(1.0)
    nx = -(-len(rows) // 16) * 16
    while len(rows) < nx:
        rows.append(zero)
        valid.append(0.0)

    local = ((tok[:, None] // HGRN_BLOCK == tok[None, :] // HGRN_BLOCK) & (tok[None, :] <= tok[:, None]))
    lhs = np.concatenate([local.astype(np.float32), np.stack(rows)], axis=0)
    valid = np.broadcast_to(np.asarray(valid, np.float32)[:, None], (nx, width)).copy()

    rb, cb = tok[:, None] // HGRN_BLOCK, tok[None, :] // HGRN_BLOCK
    cls = np.full((group, group), len(halves), np.int32)
    for li, h in enumerate(halves[:-1]):
        hb = h // HGRN_BLOCK
        own = (rb // (2 * hb) == cb // (2 * hb)) & ((rb // hb) % 2 == 1) & ((cb // hb) % 2 == 0)
        cls[own] = li + 1
    cls[local] = 0
    return lhs, valid, cls, index, halves


def _hgrn_kernel(q_ref, f_ref, i_ref, og_ref, lbl_ref, nrm_ref, s0_ref, lhs_ref, valid_ref, cls_ref,
                 y_ref, s_ref, st_ref, fc_ref, *, layer, group, seqs, seq_rows, index, halves):
    chunk = pl.program_id(2)
    width = HEADS_PER_STEP * HEAD_DIM
    nb = group // HGRN_BLOCK

    @pl.when(chunk == 0)
    def _():
        for j in range(seqs):
            for hh in range(HEADS_PER_STEP):
                st_ref[j, hh] = s0_ref[j, hh].T

    logits = lbl_ref[...]
    e = jnp.exp(logits - jnp.max(logits, axis=0, keepdims=True))
    lb = jnp.sum(e[:layer + 1], axis=0, keepdims=True) / jnp.sum(e, axis=0, keepdims=True)
    lb1 = 1.0 - lb
    gain = nrm_ref[...]
    lhs = lhs_ref[...]
    valid = valid_ref[...]
    cls = cls_ref[...]
    masks = [cls == ci for ci in range(len(halves))]

    for item in range(seqs * (seq_rows // group)):
        j, g = divmod(item, seq_rows // group)
        rs = slice(j * seq_rows + g * group, j * seq_rows + (g + 1) * group)
        q = q_ref[0, rs, :].astype(F32)
        fz = f_ref[0, rs, :].astype(F32)
        v = i_ref[0, rs, :]
        og = og_ref[0, rs, :].astype(F32)

        f = lb + lb1 * jax.nn.sigmoid(fz)
        lg = jnp.log(f) * INV_LN2
        k = 1.0 - f
        qf = q * jax.nn.sigmoid(q)

        top = pltpu.bitcast(pltpu.bitcast(lg, jnp.uint32) & jnp.uint32(0xFFFF0000), F32)
        sums = _dot(lhs, top.astype(BF16)) + _dot(lhs, (lg - top).astype(BF16))
        e_loc = sums[:group]
        fc_ref[item] = jnp.exp2(sums[group:]) * valid

        def expand(name):
            base = index[name]
            return jnp.concatenate(
                [jnp.broadcast_to(fc_ref[item, pl.ds(base + b, 1), :], (HGRN_BLOCK, width)) for b in range(nb)],
                axis=0)

        qd = qf * jnp.exp2(e_loc)
        kd = k * jnp.exp2(-e_loc)
        qd_b = qd.astype(BF16)
        kd_b = kd.astype(BF16)
        q_lv = [qd_b if h == HGRN_BLOCK else (qd * expand(("q", h))).astype(BF16) for h in halves]
        k_lv = [(kd * expand(("k", h))).astype(BF16) for h in halves]
        q_in = (qd * expand("q_in")).astype(BF16)
        k_out = (kd * expand("k_out")).astype(BF16)
        decay = fc_ref[item, pl.ds(index["decay"], 1), :]

        outs = []
        for hh in range(HEADS_PER_STEP):
            sl = slice(hh * HEAD_DIM, (hh + 1) * HEAD_DIM)
            if group % 128 == 0:
                both = _dot_nt(qd_b[:, sl], jnp.concatenate([kd_b[:, sl], k_lv[0][:, sl]], axis=0))
                parts = [both[:, :group], both[:, group:]]
            else:
                parts = [_dot_nt(qd_b[:, sl], kd_b[:, sl]), _dot_nt(qd_b[:, sl], k_lv[0][:, sl])]
            parts += [_dot_nt(ql[:, sl], kl[:, sl]) for ql, kl in zip(q_lv[1:], k_lv[1:])]
            a = parts[-1]
            for ci in reversed(range(len(halves))):
                a = jnp.where(masks[ci], parts[ci], a)
            st = st_ref[j, hh]
            o = _dot(a.astype(BF16), v[:, sl]) + _dot_nt(q_in[:, sl], st.astype(BF16))
            st_ref[j, hh] = st * decay[:, sl] + _dot_tn(v[:, sl], k_out[:, sl])
            outs.append(o * lax.rsqrt(jnp.mean(o * o, axis=-1, keepdims=True) + NORM_EPS))
        o = jnp.concatenate(outs, axis=1) * gain
        y_ref[0, rs, :] = (o * (og * jax.nn.sigmoid(og))).astype(y_ref.dtype)

    @pl.when(chunk == pl.num_programs(2) - 1)
    def _():
        for j in range(seqs):
            for hh in range(HEADS_PER_STEP):
                s_ref[j, hh] = st_ref[j, hh].T


def _hgrn(p, lb_logits, hgrn_norm, s0, *, first_group, layer, batch, seqs, tc, group):
    _, m, width = p.shape
    npairs = s0.shape[1] // HEADS_PER_STEP
    t = m // batch
    nc = t // tc
    assert seqs == 1 or nc == 1
    rows = seqs * tc
    lhs, valid, cls, index, halves = _hgrn_tables(group, width)
    row = lambda b, c: b * nc + c
    part = lambda k: pl.BlockSpec((1, rows, width), lambda b, hp, c: (first_group + k * npairs + hp, row(b, c), 0))
    state_spec = pl.BlockSpec((seqs, HEADS_PER_STEP, HEAD_DIM, HEAD_DIM), lambda b, hp, c: (b, hp, 0, 0))
    const = lambda a: pl.BlockSpec(a.shape, lambda b, hp, c: (0, 0))
    n_items = seqs * (tc // group)
    return pl.pallas_call(
        functools.partial(_hgrn_kernel, layer=layer, group=group, seqs=seqs, seq_rows=tc, index=index,
                          halves=halves),
        grid=(batch // seqs, npairs, nc),
        in_specs=[
            part(0), part(1), part(2), part(3),
            pl.BlockSpec((lb_logits.shape[0], width), lambda b, hp, c: (0, hp)),
            pl.BlockSpec((1, width), lambda b, hp, c: (0, hp)),
            state_spec, const(lhs), const(valid), const(cls),
        ],
        out_specs=[
            pl.BlockSpec((1, rows, width), lambda b, hp, c: (hp, row(b, c), 0)),
            state_spec,
        ],
        out_shape=[
            jax.ShapeDtypeStruct((npairs, m, width), BF16),
            jax.ShapeDtypeStruct(s0.shape, F32),
        ],
        scratch_shapes=[
            pltpu.VMEM((seqs, HEADS_PER_STEP, HEAD_DIM, HEAD_DIM), F32),
            pltpu.VMEM((n_items, valid.shape[0], width), F32),
        ],
        compiler_params=_params(("arbitrary", "arbitrary", "arbitrary")),
        name="hgrn2",
    )(p, p, p, p, lb_logits, hgrn_norm, s0, jnp.asarray(lhs, BF16), jnp.asarray(valid), jnp.asarray(cls))


def _merge_kernel(yc_ref, yh_ref, gc_ref, gh_ref, x_ref, wbc_ref, wbh_ref, wo_ref, o_ref):
    def branch(y_ref, w_ref):
        acc = _dot(y_ref[0], w_ref[0])
        for g in range(1, y_ref.shape[0]):
            acc = acc + _dot(y_ref[g], w_ref[g])
        return acc

    gate = lambda ref: jax.nn.sigmoid(jnp.concatenate([ref[g] for g in range(ref.shape[0])], axis=1).astype(F32))
    merged = (gate(gc_ref) * branch(yc_ref, wbc_ref) + gate(gh_ref) * branch(yh_ref, wbh_ref)).astype(BF16)
    o_ref[...] = x_ref[...] + _dot(merged, wo_ref[...])


def _merge(yc, yh, p, x, wbc, wbh, wo, *, gate_group, tm):
    m, d = x.shape
    gw = p.shape[-1]
    ngate = d // gw
    assert gate_group % ngate == 0
    const = lambda shape: pl.BlockSpec(shape, lambda i: (0,) * len(shape), pipeline_mode=pl.Buffered(1))
    rows = lambda a: pl.BlockSpec((a.shape[0], tm, gw), lambda i: (0, i, 0))
    gate = lambda k: pl.BlockSpec((ngate, tm, gw), lambda i: (gate_group // ngate + k, i, 0))
    return pl.pallas_call(
        _merge_kernel,
        grid=(m // tm,),
        in_specs=[
            rows(yc), rows(yh), gate(0), gate(1),
            pl.BlockSpec((tm, d), lambda i: (i, 0)),
            const(wbc.shape), const(wbh.shape), const(wo.shape),
        ],
        out_specs=pl.BlockSpec((tm, d), lambda i: (i, 0)),
        out_shape=jax.ShapeDtypeStruct((m, d), F32),
        compiler_params=_params(("arbitrary",)),
        name="merge_out",
    )(yc, yh, p, p, x, wbc, wbh, wo)


def _trunk(x, conv0, s0, w, *, tm_ffn, tf, tm_proj, tm_merge, tc_conv, seqs_hgrn, tc_hgrn, group):
    batch, t, d = x.shape
    m = batch * t
    xf = x.reshape(m, d)
    groups = w["in_groups"]

    x1 = _ffn(xf, w["norm_ffn1"], w["ffn1_gate"], w["ffn1_up"], w["ffn1_down"], w["norm_final"],
              final_norm=False, tm=tm_ffn, tf=tf)
    p = _proj(x1, w["norm_mix"], w["in"], group_width=LANE_GROUP, tm=tm_proj, tn=1024)

    yc, conv_new = _conv(p, conv0, w["conv_w"], first_group=groups["conv"], batch=batch, tc=tc_conv)
    yh, s_new = _hgrn(p, w["lb_logits"], w["hgrn_norm"], s0, first_group=groups["hgrn"], layer=0, batch=batch,
                      seqs=seqs_hgrn, tc=tc_hgrn, group=group)

    x2 = _merge(yc, yh, p, x1, w["br_conv"], w["br_hgrn"], w["out"], gate_group=groups["gate"], tm=tm_merge)
    y = _ffn(x2, w["norm_ffn2"], w["ffn2_gate"], w["ffn2_up"], w["ffn2_down"], w["norm_final"],
             final_norm=True, tm=tm_ffn, tf=tf)
    return y.reshape(batch, t, d), conv_new, s_new


def _prepare_weights(norm_ffn1, w_ffn1_gate, w_ffn1_up, w_ffn1_down, norm_mix, w_in, conv_w, hgrn_lb_logits, hgrn_norm,
                     w_br_conv, w_br_hgrn, w_out, norm_ffn2, w_ffn2_gate, w_ffn2_up, w_ffn2_down, norm_final, *, dc, dh):
    assert w_in.shape[0] == 1, "single-layer trunk"
    d = w_in.shape[1]
    bf = lambda a: a.astype(BF16)
    win = w_in[0]
    n_conv, n_hgrn = 3 * dc, 4 * dh
    n_gate = win.shape[1] - n_conv - n_hgrn
    win = jnp.concatenate([win[:, n_conv + n_hgrn:], win[:, :n_conv], win[:, n_conv:n_conv + n_hgrn]], axis=1)
    return dict(
        norm_ffn1=norm_ffn1, ffn1_gate=bf(w_ffn1_gate[0]), ffn1_up=bf(w_ffn1_up[0]), ffn1_down=bf(w_ffn1_down[0]),
        norm_mix=norm_mix, **{"in": bf(win)},
        in_groups=dict(gate=0, conv=n_gate // LANE_GROUP, hgrn=(n_gate + n_conv) // LANE_GROUP),
        conv_w=conv_w[0], lb_logits=hgrn_lb_logits, hgrn_norm=hgrn_norm,
        br_conv=bf(w_br_conv[0]).reshape(dc // LANE_GROUP, LANE_GROUP, d),
        br_hgrn=bf(w_br_hgrn[0]).reshape(dh // LANE_GROUP, LANE_GROUP, d), out=bf(w_out[0]),
        norm_ffn2=norm_ffn2, ffn2_gate=bf(w_ffn2_gate[0]), ffn2_up=bf(w_ffn2_up[0]), ffn2_down=bf(w_ffn2_down[0]),
        norm_final=norm_final.reshape(1, d),
    )


def kernel(x_prompt, x_sample, cache_conv, state_hgrn, norm_ffn1, w_ffn1_gate, w_ffn1_up, w_ffn1_down, norm_mix, w_in, conv_w, hgrn_lb_logits, hgrn_norm, w_br_conv, w_br_hgrn, w_out, norm_ffn2, w_ffn2_gate, w_ffn2_up, w_ffn2_down, norm_final):
    heads, dk, dv = state_hgrn.shape[-3:]
    assert dk == HEAD_DIM and dv == HEAD_DIM and heads % HEADS_PER_STEP == 0
    w = _prepare_weights(norm_ffn1, w_ffn1_gate, w_ffn1_up, w_ffn1_down, norm_mix, w_in, conv_w, hgrn_lb_logits,
                         hgrn_norm, w_br_conv, w_br_hgrn, w_out, norm_ffn2, w_ffn2_gate, w_ffn2_up, w_ffn2_down,
                         norm_final, dc=cache_conv.shape[-1], dh=heads * dk)
    bp = x_prompt.shape[0]
    zero_conv = jnp.zeros((bp,) + cache_conv.shape[2:], F32)
    zero_hgrn = jnp.zeros((bp,) + state_hgrn.shape[2:], F32)
    yp, cp, sp = _trunk(x_prompt, zero_conv, zero_hgrn, w, tm_ffn=1024, tf=512, tm_proj=1024, tm_merge=256,
                        tc_conv=512, seqs_hgrn=1, tc_hgrn=1024, group=128)
    ys, cs, ss = _trunk(x_sample, cache_conv[0], state_hgrn[0], w, tm_ffn=1024, tf=512, tm_proj=1024, tm_merge=256,
                        tc_conv=64, seqs_hgrn=16, tc_hgrn=64, group=64)
    return yp, ys, cp[None], sp[None], cs[None], ss[None]
```

```python
import functools

import numpy as np
import jax
import jax.numpy as jnp
from jax import lax
from jax.experimental import pallas as pl
from jax.experimental.pallas import tpu as pltpu

NORM_EPS = 1e-6
INV_LN2 = 1.4426950408889634
CONV_WIDTH = 3
HGRN_BLOCK = 16
HEAD_DIM = 128
HEADS_PER_STEP = 2
LANE_GROUP = HEADS_PER_STEP * HEAD_DIM
VMEM_LIMIT_BYTES = 56 * 1024 * 1024

F32 = jnp.float32
BF16 = jnp.bfloat16


def _params(semantics):
    return pltpu.CompilerParams(dimension_semantics=semantics, vmem_limit_bytes=VMEM_LIMIT_BYTES)


def _rms(x, gain):
    r = lax.rsqrt(jnp.mean(x * x, axis=-1, keepdims=True) + NORM_EPS)
    return x * r * gain


def _dot(a, b):
    return jnp.dot(a, b, preferred_element_type=F32)


def _dot_nt(a, b):
    return lax.dot_general(a, b, (((1,), (1,)), ((), ())), preferred_element_type=F32)


def _dot_tn(a, b):
    return lax.dot_general(a, b, (((0,), (0,)), ((), ())), preferred_element_type=F32)


def _ffn_kernel(x_ref, g_ref, wg_ref, wu_ref, wd_ref, gf_ref, o_ref, h_ref, *, final_norm):
    f = pl.program_id(1)

    @pl.when(f == 0)
    def _():
        x = x_ref[...]
        h_ref[...] = _rms(x, g_ref[...]).astype(BF16)
        o_ref[...] = x

    h = h_ref[...]
    g = _dot(h, wg_ref[...])
    u = _dot(h, wu_ref[...])
    a = (g * jax.nn.sigmoid(g) * u * 0.5).astype(BF16)
    o_ref[...] += _dot(a, wd_ref[...])

    if final_norm:
        @pl.when(f == pl.num_programs(1) - 1)
        def _():
            o_ref[...] = _rms(o_ref[...], gf_ref[...])


def _ffn(x, gain, wg, wu, wd, final_gain, *, final_norm, tm, tf):
    m, d = x.shape
    dff = wg.shape[1]
    grid = (m // tm, dff // tf)
    return pl.pallas_call(
        functools.partial(_ffn_kernel, final_norm=final_norm),
        grid=grid,
        in_specs=[
            pl.BlockSpec((tm, d), lambda i, f: (i, 0)),
            pl.BlockSpec((1, d), lambda i, f: (0, 0)),
            pl.BlockSpec((d, tf), lambda i, f: (0, f)),
            pl.BlockSpec((d, tf), lambda i, f: (0, f)),
            pl.BlockSpec((tf, d), lambda i, f: (f, 0)),
            pl.BlockSpec((1, d), lambda i, f: (0, 0)),
        ],
        out_specs=pl.BlockSpec((tm, d), lambda i, f: (i, 0)),
        out_shape=jax.ShapeDtypeStruct((m, d), F32),
        scratch_shapes=[pltpu.VMEM((tm, d), BF16)],
        compiler_params=_params(("arbitrary", "arbitrary")),
        name="ffn_final" if final_norm else "ffn",
    )(x, gain, wg, wu, wd, final_gain)


def _proj_kernel(x_ref, g_ref, w_ref, o_ref, h_ref, *, group_width):
    @pl.when(pl.program_id(1) == 0)
    def _():
        h_ref[...] = _rms(x_ref[...], g_ref[...]).astype(BF16)

    res = _dot(h_ref[...], w_ref[...]).astype(o_ref.dtype)
    for k in range(o_ref.shape[0]):
        o_ref[k] = res[:, k * group_width:(k + 1) * group_width]


def _proj(x, gain, w, *, group_width, tm, tn):
    m, d = x.shape
    n = w.shape[1]
    gpb = tn // group_width
    return pl.pallas_call(
        functools.partial(_proj_kernel, group_width=group_width),
        grid=(m // tm, n // tn),
        in_specs=[
            pl.BlockSpec((tm, d), lambda i, j: (i, 0)),
            pl.BlockSpec((1, d), lambda i, j: (0, 0)),
            pl.BlockSpec((d, tn), lambda i, j: (0, j)),
        ],
        out_specs=pl.BlockSpec((gpb, tm, group_width), lambda i, j: (j, i, 0)),
        out_shape=jax.ShapeDtypeStruct((n // group_width, m, group_width), BF16),
        scratch_shapes=[pltpu.VMEM((tm, d), BF16)],
        compiler_params=_params(("arbitrary", "arbitrary")),
        name="in_proj",
    )(x, gain, w)


def _conv_kernel(b_ref, c_ref, v_ref, c0_ref, w_ref, y_ref, tail_ref, ubuf, *, tc):
    @pl.when(pl.program_id(1) == 0)
    def _():
        tail_ref[...] = c0_ref[...]

    ng = y_ref.shape[0]
    wide = lambda ref: jnp.concatenate([ref[g] for g in range(ng)], axis=1).astype(F32)
    pad = 8
    u = wide(c_ref) * wide(v_ref)
    ubuf[pad - (CONV_WIDTH - 1):pad, :] = tail_ref[0]
    ubuf[pad:pad + tc, :] = u
    w = w_ref[...]
    acc = w[CONV_WIDTH - 1:CONV_WIDTH] * u
    for j in range(CONV_WIDTH - 1):
        shift = CONV_WIDTH - 1 - j
        acc = acc + w[j:j + 1] * ubuf[pad - shift:pad - shift + tc, :]
    y = (wide(b_ref) * acc).astype(y_ref.dtype)
    for g in range(ng):
        y_ref[g] = y[:, g * LANE_GROUP:(g + 1) * LANE_GROUP]
    tail_ref[0] = u[tc - (CONV_WIDTH - 1):, :]


def _conv(p, conv0, conv_w, *, first_group, batch, tc):
    _, m, gw = p.shape
    dc = conv0.shape[-1]
    ng = dc // gw
    assert first_group % ng == 0
    t = m // batch
    nc = t // tc
    row = lambda b, c: b * nc + c
    part = lambda k: pl.BlockSpec((ng, tc, gw), lambda b, c: (first_group // ng + k, row(b, c), 0))
    return pl.pallas_call(
        functools.partial(_conv_kernel, tc=tc),
        grid=(batch, nc),
        in_specs=[
            part(0), part(1), part(2),
            pl.BlockSpec((1, CONV_WIDTH - 1, dc), lambda b, c: (b, 0, 0)),
            pl.BlockSpec((CONV_WIDTH, dc), lambda b, c: (0, 0)),
        ],
        out_specs=[
            pl.BlockSpec((ng, tc, gw), lambda b, c: (0, row(b, c), 0)),
            pl.BlockSpec((1, CONV_WIDTH - 1, dc), lambda b, c: (b, 0, 0)),
        ],
        out_shape=[
            jax.ShapeDtypeStruct((ng, m, gw), BF16),
            jax.ShapeDtypeStruct((batch, CONV_WIDTH - 1, dc), F32),
        ],
        scratch_shapes=[pltpu.VMEM((tc + 8, dc), F32)],
        compiler_params=_params(("arbitrary", "arbitrary")),
        name="short_conv",
    )(p, p, p, conv0, conv_w)


def _hgrn_tables(group, width):
    nb = group // HGRN_BLOCK
    halves = []
    h = HGRN_BLOCK
    while h < group:
        halves.append(h)
        h *= 2
    assert halves, "group must span at least two blocks"
    tok = np.arange(group)

    def span(b0, b1):
        return ((tok >= b0 * HGRN_BLOCK) & (tok < b1 * HGRN_BLOCK)).astype(np.float32)

    zero = np.zeros(group, np.float32)
    rows, valid, index = [], [], {}
    for h in halves:
        hb = h // HGRN_BLOCK
        krows, kval, qrows, qval = [], [], [], []
        for b in range(nb):
            bmid = (b // (2 * hb)) * 2 * hb + hb
            first = b < bmid
            krows.append(span(b, bmid) if first else zero)
            kval.append(1.0 if first else 0.0)
            qrows.append(zero if first else span(bmid, b))
            qval.append(0.0 if first else 1.0)
        index[("k", h)] = len(rows)
        rows += krows
        valid += kval
        if hb > 1:
            index[("q", h)] = len(rows)
            rows += qrows
            valid += qval
    index["q_in"] = len(rows)
    rows += [span(0, b) for b in range(nb)]
    valid += [1.0] * nb
    index["k_out"] = len(rows)
    rows += [span(b, nb) for b in range(nb)]
    valid += [1.0] * nb
    index["decay"] = len(rows)
    rows.append(span(0, nb))
    valid.append(1.0)
    nx = -(-len(rows) // 16) * 16
    while len(rows) < nx:
        rows.append(zero)
        valid.append(0.0)

    local = ((tok[:, None] // HGRN_BLOCK == tok[None, :] // HGRN_BLOCK) & (tok[None, :] <= tok[:, None]))
    lhs = np.concatenate([local.astype(np.float32), np.stack(rows)], axis=0)
    lhs = np.concatenate([lhs, lhs], axis=1)
    valid = np.broadcast_to(np.asarray(valid, np.float32)[:, None], (nx, width)).copy()

    rb, cb = tok[:, None] // HGRN_BLOCK, tok[None, :] // HGRN_BLOCK
    cls = np.full((group, group), len(halves), np.int32)
    for li, h in enumerate(halves[:-1]):
        hb = h // HGRN_BLOCK
        own = (rb // (2 * hb) == cb // (2 * hb)) & ((rb // hb) % 2 == 1) & ((cb // hb) % 2 == 0)
        cls[own] = li + 1
    cls[local] = 0
    return lhs, valid, cls, index, halves


def _hgrn_kernel(q_ref, f_ref, i_ref, og_ref, lbl_ref, nrm_ref, s0_ref, lhs_ref, valid_ref, cls_ref,
                 y_ref, s_ref, st_ref, fc_ref, *, layer, group, seqs, seq_rows, index, halves):
    chunk = pl.program_id(2)
    width = HEADS_PER_STEP * HEAD_DIM
    nb = group // HGRN_BLOCK

    @pl.when(chunk == 0)
    def _():
        for j in range(seqs):
            for hh in range(HEADS_PER_STEP):
                st_ref[j, hh] = s0_ref[j, hh].T

    logits = lbl_ref[...]
    e = jnp.exp(logits - jnp.max(logits, axis=0, keepdims=True))
    lb = jnp.sum(e[:layer + 1], axis=0, keepdims=True) / jnp.sum(e, axis=0, keepdims=True)
    lb1 = 1.0 - lb
    gain = nrm_ref[...]
    lhs = lhs_ref[...]
    valid = valid_ref[...]
    cls = cls_ref[...]
    masks = [cls == ci for ci in range(len(halves))]

    n_items = seqs * (seq_rows // group)

    def rows_of(item):
        j, g = divmod(item, seq_rows // group)
        return j, slice(j * seq_rows + g * group, j * seq_rows + (g + 1) * group)

    def front(item):
        j, rs = rows_of(item)
        q = q_ref[0, rs, :].astype(F32)
        fz = f_ref[0, rs, :].astype(F32)
        v = i_ref[0, rs, :]
        og = og_ref[0, rs, :].astype(F32)

        f = lb + lb1 * jax.nn.sigmoid(fz)
        lg = jnp.log(f) * INV_LN2
        k = 1.0 - f
        qf = q * jax.nn.sigmoid(q)

        top = pltpu.bitcast(pltpu.bitcast(lg, jnp.uint32) & jnp.uint32(0xFFFF0000), F32)
        pieces = jnp.concatenate([top.astype(BF16), (lg - top).astype(BF16)], axis=0)
        sums = _dot(lhs, pieces)
        e_loc = sums[:group]
        fc_ref[item] = jnp.exp2(sums[group:]) * valid

        def expand(name):
            base = index[name]
            tiles = []
            for b in range(nb):
                tile = jnp.broadcast_to(fc_ref[item, pl.ds(base + b, 1), :], (8, width))
                tiles.append(jnp.concatenate([tile] * (HGRN_BLOCK // 8), axis=0).astype(BF16))
            return jnp.concatenate(tiles, axis=0)

        qd_b = (qf * jnp.exp2(e_loc)).astype(BF16)
        kd_b = (k * jnp.exp2(-e_loc)).astype(BF16)
        q_lv = [qd_b if h == HGRN_BLOCK else qd_b * expand(("q", h)) for h in halves]
        k_lv = [kd_b * expand(("k", h)) for h in halves]
        q_in = qd_b * expand("q_in")
        k_out = kd_b * expand("k_out")
        sog = (og * jax.nn.sigmoid(og)) * gain
        return qd_b, kd_b, q_lv, k_lv, q_in, k_out, v, sog

    def mid(ops):
        qd_b, kd_b, q_lv, k_lv, q_in, k_out, v, sog = ops
        a_heads = []
        for hh in range(HEADS_PER_STEP):
            sl = slice(hh * HEAD_DIM, (hh + 1) * HEAD_DIM)
            if group % 128 == 0:
                both = _dot_nt(qd_b[:, sl], jnp.concatenate([kd_b[:, sl], k_lv[0][:, sl]], axis=0))
                parts = [both[:, :group], both[:, group:]]
            else:
                parts = [_dot_nt(qd_b[:, sl], kd_b[:, sl]), _dot_nt(qd_b[:, sl], k_lv[0][:, sl])]
            parts += [_dot_nt(ql[:, sl], kl[:, sl]) for ql, kl in zip(q_lv[1:], k_lv[1:])]
            a = parts[-1]
            for ci in reversed(range(len(halves))):
                a = jnp.where(masks[ci], parts[ci], a)
            a_heads.append(a.astype(BF16))
        return a_heads, q_in, k_out, v, sog

    def back(item, ops):
        a_heads, q_in, k_out, v, sog = ops
        j, rs = rows_of(item)
        decay = fc_ref[item, pl.ds(index["decay"], 1), :]
        outs = []
        for hh in range(HEADS_PER_STEP):
            sl = slice(hh * HEAD_DIM, (hh + 1) * HEAD_DIM)
            st = st_ref[j, hh]
            o = _dot(a_heads[hh], v[:, sl]) + _dot_nt(q_in[:, sl], st.astype(BF16))
            st_ref[j, hh] = st * decay[:, sl] + _dot_tn(v[:, sl], k_out[:, sl])
            outs.append(o * lax.rsqrt(jnp.mean(o * o, axis=-1, keepdims=True) + NORM_EPS))
        y_ref[0, rs, :] = (jnp.concatenate(outs, axis=1) * sog).astype(y_ref.dtype)

    fronts, mids = {}, {}
    for t in range(n_items + 2):
        if t < n_items:
            fronts[t] = front(t)
        if 0 <= t - 1 < n_items:
            mids[t - 1] = mid(fronts.pop(t - 1))
        if 0 <= t - 2 < n_items:
            back(t - 2, mids.pop(t - 2))

    @pl.when(chunk == pl.num_programs(2) - 1)
    def _():
        for j in range(seqs):
            for hh in range(HEADS_PER_STEP):
                s_ref[j, hh] = st_ref[j, hh].T


def _hgrn(p, lb_logits, hgrn_norm, s0, *, first_group, layer, batch, seqs, tc, group):
    _, m, width = p.shape
    npairs = s0.shape[1] // HEADS_PER_STEP
    t = m // batch
    nc = t // tc
    assert seqs == 1 or nc == 1
    rows = seqs * tc
    lhs, valid, cls, index, halves = _hgrn_tables(group, width)
    row = lambda b, c: b * nc + c
    part = lambda k: pl.BlockSpec((1, rows, width), lambda b, hp, c: (first_group + k * npairs + hp, row(b, c), 0))
    state_spec = pl.BlockSpec((seqs, HEADS_PER_STEP, HEAD_DIM, HEAD_DIM), lambda b, hp, c: (b, hp, 0, 0))
    const = lambda a: pl.BlockSpec(a.shape, lambda b, hp, c: (0, 0))
    n_items = seqs * (tc // group)
    return pl.pallas_call(
        functools.partial(_hgrn_kernel, layer=layer, group=group, seqs=seqs, seq_rows=tc, index=index,
                          halves=halves),
        grid=(batch // seqs, npairs, nc),
        in_specs=[
            part(0), part(1), part(2), part(3),
            pl.BlockSpec((lb_logits.shape[0], width), lambda b, hp, c: (0, hp)),
            pl.BlockSpec((1, width), lambda b, hp, c: (0, hp)),
            state_spec, const(lhs), const(valid), const(cls),
        ],
        out_specs=[
            pl.BlockSpec((1, rows, width), lambda b, hp, c: (hp, row(b, c), 0)),
            state_spec,
        ],
        out_shape=[
            jax.ShapeDtypeStruct((npairs, m, width), BF16),
            jax.ShapeDtypeStruct(s0.shape, F32),
        ],
        scratch_shapes=[
            pltpu.VMEM((seqs, HEADS_PER_STEP, HEAD_DIM, HEAD_DIM), F32),
            pltpu.VMEM((n_items, valid.shape[0], width), F32),
        ],
        compiler_params=_params(("arbitrary", "arbitrary", "arbitrary")),
        name="hgrn2",
    )(p, p, p, p, lb_logits, hgrn_norm, s0, jnp.asarray(lhs, BF16), jnp.asarray(valid), jnp.asarray(cls))


def _merge_kernel(yc_ref, yh_ref, gc_ref, gh_ref, x_ref, wbc_ref, wbh_ref, wo_ref, o_ref):
    def branch(y_ref, w_ref):
        acc = _dot(y_ref[0], w_ref[0])
        for g in range(1, y_ref.shape[0]):
            acc = acc + _dot(y_ref[g], w_ref[g])
        return acc

    gate = lambda ref: jax.nn.sigmoid(jnp.concatenate([ref[g] for g in range(ref.shape[0])], axis=1).astype(F32))
    merged = (gate(gc_ref) * branch(yc_ref, wbc_ref) + gate(gh_ref) * branch(yh_ref, wbh_ref)).astype(BF16)
    o_ref[...] = x_ref[...] + _dot(merged, wo_ref[...])


def _merge(yc, yh, p, x, wbc, wbh, wo, *, gate_group, tm):
    m, d = x.shape
    gw = p.shape[-1]
    ngate = d // gw
    assert gate_group % ngate == 0
    const = lambda shape: pl.BlockSpec(shape, lambda i: (0,) * len(shape), pipeline_mode=pl.Buffered(1))
    rows = lambda a: pl.BlockSpec((a.shape[0], tm, gw), lambda i: (0, i, 0))
    gate = lambda k: pl.BlockSpec((ngate, tm, gw), lambda i: (gate_group // ngate + k, i, 0))
    return pl.pallas_call(
        _merge_kernel,
        grid=(m // tm,),
        in_specs=[
            rows(yc), rows(yh), gate(0), gate(1),
            pl.BlockSpec((tm, d), lambda i: (i, 0)),
            const(wbc.shape), const(wbh.shape), const(wo.shape),
        ],
        out_specs=pl.BlockSpec((tm, d), lambda i: (i, 0)),
        out_shape=jax.ShapeDtypeStruct((m, d), F32),
        compiler_params=_params(("arbitrary",)),
        name="merge_out",
    )(yc, yh, p, p, x, wbc, wbh, wo)


def _trunk(x, conv0, s0, w, *, tm_ffn, tf, tm_proj, tm_merge, tc_conv, seqs_hgrn, tc_hgrn, group):
    batch, t, d = x.shape
    m = batch * t
    xf = x.reshape(m, d)
    groups = w["in_groups"]

    x1 = _ffn(xf, w["norm_ffn1"], w["ffn1_gate"], w["ffn1_up"], w["ffn1_down"], w["norm_final"],
              final_norm=False, tm=tm_ffn, tf=tf)
    p = _proj(x1, w["norm_mix"], w["in"], group_width=LANE_GROUP, tm=tm_proj, tn=2560)

    yc, conv_new = _conv(p, conv0, w["conv_w"], first_group=groups["conv"], batch=batch, tc=tc_conv)
    yh, s_new = _hgrn(p, w["lb_logits"], w["hgrn_norm"], s0, first_group=groups["hgrn"], layer=0, batch=batch,
                      seqs=seqs_hgrn, tc=tc_hgrn, group=group)

    x2 = _merge(yc, yh, p, x1, w["br_conv"], w["br_hgrn"], w["out"], gate_group=groups["gate"], tm=tm_merge)
    y = _ffn(x2, w["norm_ffn2"], w["ffn2_gate"], w["ffn2_up"], w["ffn2_down"], w["norm_final"],
             final_norm=True, tm=tm_ffn, tf=tf)
    return y.reshape(batch, t, d), conv_new, s_new


def _prepare_weights(norm_ffn1, w_ffn1_gate, w_ffn1_up, w_ffn1_down, norm_mix, w_in, conv_w, hgrn_lb_logits, hgrn_norm,
                     w_br_conv, w_br_hgrn, w_out, norm_ffn2, w_ffn2_gate, w_ffn2_up, w_ffn2_down, norm_final, *, dc, dh):
    assert w_in.shape[0] == 1, "single-layer trunk"
    d = w_in.shape[1]
    bf = lambda a: a.astype(BF16)
    win = w_in[0]
    n_conv, n_hgrn = 3 * dc, 4 * dh
    n_gate = win.shape[1] - n_conv - n_hgrn
    win = jnp.concatenate([win[:, n_conv + n_hgrn:], win[:, :n_conv], win[:, n_conv:n_conv + n_hgrn]], axis=1)
    return dict(
        norm_ffn1=norm_ffn1, ffn1_gate=bf(w_ffn1_gate[0]), ffn1_up=bf(w_ffn1_up[0]), ffn1_down=bf(w_ffn1_down[0]),
        norm_mix=norm_mix, **{"in": bf(win)},
        in_groups=dict(gate=0, conv=n_gate // LANE_GROUP, hgrn=(n_gate + n_conv) // LANE_GROUP),
        conv_w=conv_w[0], lb_logits=hgrn_lb_logits, hgrn_norm=hgrn_norm,
        br_conv=bf(w_br_conv[0]).reshape(dc // LANE_GROUP, LANE_GROUP, d),
        br_hgrn=bf(w_br_hgrn[0]).reshape(dh // LANE_GROUP, LANE_GROUP, d), out=bf(w_out[0]),
        norm_ffn2=norm_ffn2, ffn2_gate=bf(w_ffn2_gate[0]), ffn2_up=bf(w_ffn2_up[0]), ffn2_down=bf(w_ffn2_down[0]),
        norm_final=norm_final.reshape(1, d),
    )


def kernel(x_prompt, x_sample, cache_conv, state_hgrn, norm_ffn1, w_ffn1_gate, w_ffn1_up, w_ffn1_down, norm_mix, w_in, conv_w, hgrn_lb_logits, hgrn_norm, w_br_conv, w_br_hgrn, w_out, norm_ffn2, w_ffn2_gate, w_ffn2_up, w_ffn2_down, norm_final):
    heads, dk, dv = state_hgrn.shape[-3:]
    assert dk == HEAD_DIM and dv == HEAD_DIM and heads % HEADS_PER_STEP == 0
    w = _prepare_weights(norm_ffn1, w_ffn1_gate, w_ffn1_up, w_ffn1_down, norm_mix, w_in, conv_w, hgrn_lb_logits,
                         hgrn_norm, w_br_conv, w_br_hgrn, w_out, norm_ffn2, w_ffn2_gate, w_ffn2_up, w_ffn2_down,
                         norm_final, dc=cache_conv.shape[-1], dh=heads * dk)
    bp = x_prompt.shape[0]
    zero_conv = jnp.zeros((bp,) + cache_conv.shape[2:], F32)
    zero_hgrn = jnp.zeros((bp,) + state_hgrn.shape[2:], F32)
    yp, cp, sp = _trunk(x_prompt, zero_conv, zero_hgrn, w, tm_ffn=1024, tf=512, tm_proj=1024, tm_merge=256,
                        tc_conv=512, seqs_hgrn=1, tc_hgrn=4096, group=128)
    ys, cs, ss = _trunk(x_sample, cache_conv[0], state_hgrn[0], w, tm_ffn=1024, tf=512, tm_proj=1024, tm_merge=256,
                        tc_conv=64, seqs_hgrn=16, tc_hgrn=64, group=64)
    return yp, ys, cp[None], sp[None], cs[None], ss[None]
```

```python
import functools

import numpy as np
import jax
import jax.numpy as jnp
from jax import lax
from jax.experimental import pallas as pl
from jax.experimental.pallas import tpu as pltpu

NORM_EPS = 1e-6
INV_LN2 = 1.4426950408889634
CONV_WIDTH = 3
HGRN_BLOCK = 16
HEAD_DIM = 128
HEADS_PER_STEP = 2
LANE_GROUP = HEADS_PER_STEP * HEAD_DIM
VMEM_LIMIT_BYTES = 56 * 1024 * 1024

F32 = jnp.float32
BF16 = jnp.bfloat16


def _params(semantics):
    return pltpu.CompilerParams(dimension_semantics=semantics, vmem_limit_bytes=VMEM_LIMIT_BYTES)


def _rms(x, gain):
    r = lax.rsqrt(jnp.mean(x * x, axis=-1, keepdims=True) + NORM_EPS)
    return x * r * gain


def _dot(a, b):
    return jnp.dot(a, b, preferred_element_type=F32)


def _dot_nt(a, b):
    return lax.dot_general(a, b, (((1,), (1,)), ((), ())), preferred_element_type=F32)


def _dot_tn(a, b):
    return lax.dot_general(a, b, (((0,), (0,)), ((), ())), preferred_element_type=F32)


CAST_LANES = 128
CAST_ROW_MULTIPLE = 16


def _cast_streams(arrays, n_blocks_max, block_of_step):
    specs, args, shapes = [], [], []
    for a in arrays:
        n = next(k for k in range(n_blocks_max, 0, -1) if a.size % (k * CAST_LANES * CAST_ROW_MULTIPLE) == 0)
        rows = a.size // (n * CAST_LANES)
        specs.append(pl.BlockSpec((1, rows, CAST_LANES), block_of_step(n)))
        args.append(a.reshape(n, rows, CAST_LANES))
        shapes.append(jax.ShapeDtypeStruct((n, rows, CAST_LANES), BF16))
    return specs, args, shapes


def _cast_blocks(in_refs, out_refs):
    for src_ref, dst_ref in zip(in_refs, out_refs):
        dst_ref[...] = src_ref[...].astype(BF16)


def _ffn_kernel(*refs, final_norm, n_cast):
    x_ref, g_ref, wg_ref, wu_ref, wd_ref, gf_ref = refs[:6]
    cast_in = refs[6:6 + n_cast]
    o_ref = refs[6 + n_cast]
    cast_out = refs[7 + n_cast:7 + 2 * n_cast]
    h_ref = refs[7 + 2 * n_cast]
    f = pl.program_id(1)

    @pl.when(f == 0)
    def _():
        x = x_ref[...]
        h_ref[...] = _rms(x, g_ref[...]).astype(BF16)
        o_ref[...] = x

    h = h_ref[...]
    g = _dot(h, wg_ref[...])
    u = _dot(h, wu_ref[...])
    a = (g * jax.nn.sigmoid(g) * u * 0.5).astype(BF16)
    o_ref[...] += _dot(a, wd_ref[...])

    _cast_blocks(cast_in, cast_out)

    if final_norm:
        @pl.when(f == pl.num_programs(1) - 1)
        def _():
            o_ref[...] = _rms(o_ref[...], gf_ref[...])


def _ffn(x, gain, wg, wu, wd, final_gain, *, final_norm, tm, tf, cast=()):
    m, d = x.shape
    dff = wg.shape[1]
    n_i, n_f = m // tm, dff // tf
    step_block = lambda n: (lambda i, f: (jnp.minimum(i * n_f + f, n - 1), 0, 0))
    cast_specs, cast_args, cast_shapes = _cast_streams(cast, n_i * n_f, step_block)
    outs = pl.pallas_call(
        functools.partial(_ffn_kernel, final_norm=final_norm, n_cast=len(cast)),
        grid=(n_i, n_f),
        in_specs=[
            pl.BlockSpec((tm, d), lambda i, f: (i, 0)),
            pl.BlockSpec((1, d), lambda i, f: (0, 0)),
            pl.BlockSpec((d, tf), lambda i, f: (0, f)),
            pl.BlockSpec((d, tf), lambda i, f: (0, f)),
            pl.BlockSpec((tf, d), lambda i, f: (f, 0)),
            pl.BlockSpec((1, d), lambda i, f: (0, 0)),
        ] + cast_specs,
        out_specs=[pl.BlockSpec((tm, d), lambda i, f: (i, 0))] + cast_specs,
        out_shape=[jax.ShapeDtypeStruct((m, d), F32)] + cast_shapes,
        scratch_shapes=[pltpu.VMEM((tm, d), BF16)],
        compiler_params=_params(("arbitrary", "arbitrary")),
        name="ffn_final" if final_norm else "ffn",
    )(x, gain, wg, wu, wd, final_gain, *cast_args)
    return outs[0], [o.reshape(a.shape) for o, a in zip(outs[1:], cast)]


def _proj_kernel(x_ref, g_ref, w_ref, o_ref, h_ref, *, group_width):
    @pl.when(pl.program_id(1) == 0)
    def _():
        h_ref[...] = _rms(x_ref[...], g_ref[...]).astype(BF16)

    res = _dot(h_ref[...], w_ref[...]).astype(o_ref.dtype)
    for k in range(o_ref.shape[0]):
        o_ref[k] = res[:, k * group_width:(k + 1) * group_width]


def _proj(x, gain, w, *, group_width, tm, tn):
    m, d = x.shape
    n = w.shape[1]
    gpb = tn // group_width
    return pl.pallas_call(
        functools.partial(_proj_kernel, group_width=group_width),
        grid=(m // tm, n // tn),
        in_specs=[
            pl.BlockSpec((tm, d), lambda i, j: (i, 0)),
            pl.BlockSpec((1, d), lambda i, j: (0, 0)),
            pl.BlockSpec((d, tn), lambda i, j: (0, j)),
        ],
        out_specs=pl.BlockSpec((gpb, tm, group_width), lambda i, j: (j, i, 0)),
        out_shape=jax.ShapeDtypeStruct((n // group_width, m, group_width), BF16),
        scratch_shapes=[pltpu.VMEM((tm, d), BF16)],
        compiler_params=_params(("arbitrary", "arbitrary")),
        name="in_proj",
    )(x, gain, w)


def _conv_kernel(b_ref, c_ref, v_ref, c0_ref, w_ref, y_ref, tail_ref, ubuf, *, tc):
    @pl.when(pl.program_id(1) == 0)
    def _():
        tail_ref[...] = c0_ref[...]

    ng = y_ref.shape[0]
    wide = lambda ref: jnp.concatenate([ref[g] for g in range(ng)], axis=1).astype(F32)
    pad = 8
    u = wide(c_ref) * wide(v_ref)
    ubuf[pad - (CONV_WIDTH - 1):pad, :] = tail_ref[0]
    ubuf[pad:pad + tc, :] = u
    w = w_ref[...]
    acc = w[CONV_WIDTH - 1:CONV_WIDTH] * u
    for j in range(CONV_WIDTH - 1):
        shift = CONV_WIDTH - 1 - j
        acc = acc + w[j:j + 1] * ubuf[pad - shift:pad - shift + tc, :]
    y = (wide(b_ref) * acc).astype(y_ref.dtype)
    for g in range(ng):
        y_ref[g] = y[:, g * LANE_GROUP:(g + 1) * LANE_GROUP]
    tail_ref[0] = u[tc - (CONV_WIDTH - 1):, :]


def _conv(p, conv0, conv_w, *, first_group, batch, tc):
    _, m, gw = p.shape
    dc = conv0.shape[-1]
    ng = dc // gw
    assert first_group % ng == 0
    t = m // batch
    nc = t // tc
    row = lambda b, c: b * nc + c
    part = lambda k: pl.BlockSpec((ng, tc, gw), lambda b, c: (first_group // ng + k, row(b, c), 0))
    return pl.pallas_call(
        functools.partial(_conv_kernel, tc=tc),
        grid=(batch, nc),
        in_specs=[
            part(0), part(1), part(2),
            pl.BlockSpec((1, CONV_WIDTH - 1, dc), lambda b, c: (b, 0, 0)),
            pl.BlockSpec((CONV_WIDTH, dc), lambda b, c: (0, 0)),
        ],
        out_specs=[
            pl.BlockSpec((ng, tc, gw), lambda b, c: (0, row(b, c), 0)),
            pl.BlockSpec((1, CONV_WIDTH - 1, dc), lambda b, c: (b, 0, 0)),
        ],
        out_shape=[
            jax.ShapeDtypeStruct((ng, m, gw), BF16),
            jax.ShapeDtypeStruct((batch, CONV_WIDTH - 1, dc), F32),
        ],
        scratch_shapes=[pltpu.VMEM((tc + 8, dc), F32)],
        compiler_params=_params(("arbitrary", "arbitrary")),
        name="short_conv",
    )(p, p, p, conv0, conv_w)


def _hgrn_tables(group, width):
    nb = group // HGRN_BLOCK
    halves = []
    h = HGRN_BLOCK
    while h < group:
        halves.append(h)
        h *= 2
    assert halves, "group must span at least two blocks"
    tok = np.arange(group)

    def span(b0, b1):
        return ((tok >= b0 * HGRN_BLOCK) & (tok < b1 * HGRN_BLOCK)).astype(np.float32)

    zero = np.zeros(group, np.float32)
    rows, valid, index = [], [], {}
    for h in halves:
        hb = h // HGRN_BLOCK
        krows, kval, qrows, qval = [], [], [], []
        for b in range(nb):
            bmid = (b // (2 * hb)) * 2 * hb + hb
            first = b < bmid
            krows.append(span(b, bmid) if first else zero)
            kval.append(1.0 if first else 0.0)
            qrows.append(zero if first else span(bmid, b))
            qval.append(0.0 if first else 1.0)
        index[("k", h)] = len(rows)
        rows += krows
        valid += kval
        if hb > 1:
            index[("q", h)] = len(rows)
            rows += qrows
            valid += qval
    index["q_in"] = len(rows)
    rows += [span(0, b) for b in range(nb)]
    valid += [1.0] * nb
    index["k_out"] = len(rows)
    rows += [span(b, nb) for b in range(nb)]
    valid += [1.0] * nb
    index["decay"] = len(rows)
    rows.append(span(0, nb))
    valid.append(1.0)
    nx = -(-len(rows) // 16) * 16
    while len(rows) < nx:
        rows.append(zero)
        valid.append(0.0)

    local = ((tok[:, None] // HGRN_BLOCK == tok[None, :] // HGRN_BLOCK) & (tok[None, :] <= tok[:, None]))
    lhs = np.concatenate([local.astype(np.float32), np.stack(rows)], axis=0)
    lhs = np.concatenate([lhs, lhs], axis=1)
    valid = np.broadcast_to(np.asarray(valid, np.float32)[:, None], (nx, width)).copy()

    rb, cb = tok[:, None] // HGRN_BLOCK, tok[None, :] // HGRN_BLOCK
    cls = np.full((group, group), len(halves), np.int32)
    for li, h in enumerate(halves[:-1]):
        hb = h // HGRN_BLOCK
        own = (rb // (2 * hb) == cb // (2 * hb)) & ((rb // hb) % 2 == 1) & ((cb // hb) % 2 == 0)
        cls[own] = li + 1
    cls[local] = 0
    return lhs, valid, cls, index, halves


def _hgrn_kernel(*refs, layer, group, seqs, seq_rows, index, halves, n_cast):
    q_ref, f_ref, i_ref, og_ref, lbl_ref, nrm_ref, s0_ref, lhs_ref, valid_ref, cls_ref = refs[:10]
    cast_in = refs[10:10 + n_cast]
    y_ref, s_ref = refs[10 + n_cast:12 + n_cast]
    cast_out = refs[12 + n_cast:12 + 2 * n_cast]
    st_ref, fc_ref = refs[12 + 2 * n_cast:]
    chunk = pl.program_id(2)
    width = HEADS_PER_STEP * HEAD_DIM
    nb = group // HGRN_BLOCK
    _cast_blocks(cast_in, cast_out)

    @pl.when(chunk == 0)
    def _():
        for j in range(seqs):
            for hh in range(HEADS_PER_STEP):
                st_ref[j, hh] = s0_ref[j, hh].T

    logits = lbl_ref[...]
    e = jnp.exp(logits - jnp.max(logits, axis=0, keepdims=True))
    lb = jnp.sum(e[:layer + 1], axis=0, keepdims=True) / jnp.sum(e, axis=0, keepdims=True)
    lb1 = 1.0 - lb
    gain = nrm_ref[...]
    lhs = lhs_ref[...]
    valid = valid_ref[...]
    cls = cls_ref[...]
    masks = [cls == ci for ci in range(len(halves))]

    n_items = seqs * (seq_rows // group)

    def rows_of(item):
        j, g = divmod(item, seq_rows // group)
        return j, slice(j * seq_rows + g * group, j * seq_rows + (g + 1) * group)

    def front(item):
        j, rs = rows_of(item)
        q = q_ref[0, rs, :].astype(F32)
        fz = f_ref[0, rs, :].astype(F32)
        v = i_ref[0, rs, :]
        og = og_ref[0, rs, :].astype(F32)

        f = lb + lb1 * jax.nn.sigmoid(fz)
        lg = jnp.log(f) * INV_LN2
        k = 1.0 - f
        qf = q * jax.nn.sigmoid(q)

        top = pltpu.bitcast(pltpu.bitcast(lg, jnp.uint32) & jnp.uint32(0xFFFF0000), F32)
        pieces = jnp.concatenate([top.astype(BF16), (lg - top).astype(BF16)], axis=0)
        sums = _dot(lhs, pieces)
        e_loc = sums[:group]
        fc_ref[item] = jnp.exp2(sums[group:]) * valid

        def expand(name):
            base = index[name]
            tiles = []
            for b in range(nb):
                tile = jnp.broadcast_to(fc_ref[item, pl.ds(base + b, 1), :], (8, width))
                tiles.append(jnp.concatenate([tile] * (HGRN_BLOCK // 8), axis=0).astype(BF16))
            return jnp.concatenate(tiles, axis=0)

        qd_b = (qf * jnp.exp2(e_loc)).astype(BF16)
        kd_b = (k * jnp.exp2(-e_loc)).astype(BF16)
        q_lv = [qd_b if h == HGRN_BLOCK else qd_b * expand(("q", h)) for h in halves]
        k_lv = [kd_b * expand(("k", h)) for h in halves]
        q_in = qd_b * expand("q_in")
        k_out = kd_b * expand("k_out")
        sog = (og * jax.nn.sigmoid(og)) * gain
        return qd_b, kd_b, q_lv, k_lv, q_in, k_out, v, sog

    def mid(ops):
        qd_b, kd_b, q_lv, k_lv, q_in, k_out, v, sog = ops
        a_heads = []
        for hh in range(HEADS_PER_STEP):
            sl = slice(hh * HEAD_DIM, (hh + 1) * HEAD_DIM)
            if group % 128 == 0:
                both = _dot_nt(qd_b[:, sl], jnp.concatenate([kd_b[:, sl], k_lv[0][:, sl]], axis=0))
                parts = [both[:, :group], both[:, group:]]
            else:
                parts = [_dot_nt(qd_b[:, sl], kd_b[:, sl]), _dot_nt(qd_b[:, sl], k_lv[0][:, sl])]
            parts += [_dot_nt(ql[:, sl], kl[:, sl]) for ql, kl in zip(q_lv[1:], k_lv[1:])]
            a = parts[-1]
            for ci in reversed(range(len(halves))):
                a = jnp.where(masks[ci], parts[ci], a)
            a_heads.append(a.astype(BF16))
        return a_heads, q_in, k_out, v, sog

    def back(item, ops):
        a_heads, q_in, k_out, v, sog = ops
        j, rs = rows_of(item)
        decay = fc_ref[item, pl.ds(index["decay"], 1), :]
        outs = []
        for hh in range(HEADS_PER_STEP):
            sl = slice(hh * HEAD_DIM, (hh + 1) * HEAD_DIM)
            st = st_ref[j, hh]
            o = _dot(a_heads[hh], v[:, sl]) + _dot_nt(q_in[:, sl], st.astype(BF16))
            st_ref[j, hh] = st * decay[:, sl] + _dot_tn(v[:, sl], k_out[:, sl])
            outs.append(o * lax.rsqrt(jnp.mean(o * o, axis=-1, keepdims=True) + NORM_EPS))
        y_ref[0, rs, :] = (jnp.concatenate(outs, axis=1) * sog).astype(y_ref.dtype)

    fronts, mids = {}, {}
    for t in range(n_items + 2):
        if t < n_items:
            fronts[t] = front(t)
        if 0 <= t - 1 < n_items:
            mids[t - 1] = mid(fronts.pop(t - 1))
        if 0 <= t - 2 < n_items:
            back(t - 2, mids.pop(t - 2))

    @pl.when(chunk == pl.num_programs(2) - 1)
    def _():
        for j in range(seqs):
            for hh in range(HEADS_PER_STEP):
                s_ref[j, hh] = st_ref[j, hh].T


def _hgrn(p, lb_logits, hgrn_norm, s0, *, first_group, layer, batch, seqs, tc, group, cast=()):
    _, m, width = p.shape
    npairs = s0.shape[1] // HEADS_PER_STEP
    t = m // batch
    nc = t // tc
    assert seqs == 1 or nc == 1
    rows = seqs * tc
    lhs, valid, cls, index, halves = _hgrn_tables(group, width)
    row = lambda b, c: b * nc + c
    part = lambda k: pl.BlockSpec((1, rows, width), lambda b, hp, c: (first_group + k * npairs + hp, row(b, c), 0))
    state_spec = pl.BlockSpec((seqs, HEADS_PER_STEP, HEAD_DIM, HEAD_DIM), lambda b, hp, c: (b, hp, 0, 0))
    const = lambda a: pl.BlockSpec(a.shape, lambda b, hp, c: (0, 0))
    n_items = seqs * (tc // group)
    grid = (batch // seqs, npairs, nc)
    step_block = lambda n: (lambda b, hp, c: (jnp.minimum((b * npairs + hp) * nc + c, n - 1), 0, 0))
    cast_specs, cast_args, cast_shapes = _cast_streams(cast, grid[0] * grid[1] * grid[2], step_block)
    outs = pl.pallas_call(
        functools.partial(_hgrn_kernel, layer=layer, group=group, seqs=seqs, seq_rows=tc, index=index,
                          halves=halves, n_cast=len(cast)),
        grid=grid,
        in_specs=[
            part(0), part(1), part(2), part(3),
            pl.BlockSpec((lb_logits.shape[0], width), lambda b, hp, c: (0, hp)),
            pl.BlockSpec((1, width), lambda b, hp, c: (0, hp)),
            state_spec, const(lhs), const(valid), const(cls),
        ] + cast_specs,
        out_specs=[
            pl.BlockSpec((1, rows, width), lambda b, hp, c: (hp, row(b, c), 0)),
            state_spec,
        ] + cast_specs,
        out_shape=[
            jax.ShapeDtypeStruct((npairs, m, width), BF16),
            jax.ShapeDtypeStruct(s0.shape, F32),
        ] + cast_shapes,
        scratch_shapes=[
            pltpu.VMEM((seqs, HEADS_PER_STEP, HEAD_DIM, HEAD_DIM), F32),
            pltpu.VMEM((n_items, valid.shape[0], width), F32),
        ],
        compiler_params=_params(("arbitrary", "arbitrary", "arbitrary")),
        name="hgrn2",
    )(p, p, p, p, lb_logits, hgrn_norm, s0, jnp.asarray(lhs, BF16), jnp.asarray(valid), jnp.asarray(cls), *cast_args)
    return outs[0], outs[1], [o.reshape(a.shape) for o, a in zip(outs[2:], cast)]


def _merge_kernel(yc_ref, yh_ref, gc0_ref, gc1_ref, gh0_ref, gh1_ref, x_ref, wbc_ref, wbh_ref, wo_ref, o_ref):
    def branch(y_ref, w_ref):
        acc = _dot(y_ref[0], w_ref[0])
        for g in range(1, y_ref.shape[0]):
            acc = acc + _dot(y_ref[g], w_ref[g])
        return acc

    def gate(*halves):
        return jax.nn.sigmoid(jnp.concatenate([r[g] for r in halves for g in range(r.shape[0])], axis=1).astype(F32))

    merged = (gate(gc0_ref, gc1_ref) * branch(yc_ref, wbc_ref)
              + gate(gh0_ref, gh1_ref) * branch(yh_ref, wbh_ref)).astype(BF16)
    o_ref[...] = x_ref[...] + _dot(merged, wo_ref[...])


def _merge(yc, yh, p, x, wbc, wbh, wo, *, gate_group, tm):
    m, d = x.shape
    gw = p.shape[-1]
    nhalf = d // gw // 2
    assert gate_group % nhalf == 0
    const = lambda shape: pl.BlockSpec(shape, lambda i: (0,) * len(shape), pipeline_mode=pl.Buffered(1))
    rows = lambda a: pl.BlockSpec((a.shape[0], tm, gw), lambda i: (0, i, 0))
    gate = lambda k: pl.BlockSpec((nhalf, tm, gw), lambda i: (gate_group // nhalf + k, i, 0))
    return pl.pallas_call(
        _merge_kernel,
        grid=(m // tm,),
        in_specs=[
            rows(yc), rows(yh), gate(0), gate(1), gate(2), gate(3),
            pl.BlockSpec((tm, d), lambda i: (i, 0)),
            const(wbc.shape), const(wbh.shape), const(wo.shape),
        ],
        out_specs=pl.BlockSpec((tm, d), lambda i: (i, 0)),
        out_shape=jax.ShapeDtypeStruct((m, d), F32),
        compiler_params=_params(("arbitrary",)),
        name="merge_out",
    )(yc, yh, p, p, p, p, x, wbc, wbh, wo)


def _mix_and_ffn2(x1, conv0, s0, w, later, *, batch, tm_ffn, tf, tm_proj, tm_merge, tc_conv, seqs_hgrn, tc_hgrn, group):
    d = x1.shape[1]
    groups = w["in_groups"]
    p = _proj(x1, w["norm_mix"], w["in"], group_width=LANE_GROUP, tm=tm_proj, tn=2560)
    yc, conv_new = _conv(p, conv0, w["conv_w"], first_group=groups["conv"], batch=batch, tc=tc_conv)
    pending = [a for a in later if a.dtype != BF16]
    yh, s_new, done = _hgrn(p, w["lb_logits"], w["hgrn_norm"], s0, first_group=groups["hgrn"], layer=0, batch=batch,
                            seqs=seqs_hgrn, tc=tc_hgrn, group=group, cast=pending)
    later = tuple(done) if pending else later
    wbc, wbh, wo, w2g, w2u, w2d = later
    grouped = lambda a: a.reshape(a.shape[0] // LANE_GROUP, LANE_GROUP, d)
    x2 = _merge(yc, yh, p, x1, grouped(wbc), grouped(wbh), wo, gate_group=groups["gate"], tm=tm_merge)
    y, _ = _ffn(x2, w["norm_ffn2"], w2g, w2u, w2d, w["norm_final"], final_norm=True, tm=tm_ffn, tf=tf)
    return y, conv_new, s_new, later


def _forward(x_prompt, x_sample, cache_conv, state_hgrn, norm_ffn1, w_ffn1_gate, w_ffn1_up, w_ffn1_down, norm_mix, w_in,
             conv_w, hgrn_lb_logits, hgrn_norm, w_br_conv, w_br_hgrn, w_out, norm_ffn2, w_ffn2_gate, w_ffn2_up,
             w_ffn2_down, norm_final, *, prompt_tiles, sample_tiles):
    assert w_in.shape[0] == 1, "single-layer trunk"
    heads, dk, dv = state_hgrn.shape[-3:]
    assert dk == HEAD_DIM and dv == HEAD_DIM and heads % HEADS_PER_STEP == 0
    d = x_prompt.shape[-1]
    dc, dh = cache_conv.shape[-1], heads * dk
    bf = lambda a: a.astype(BF16)
    gf = norm_final.reshape(1, d)
    ffn1 = (norm_ffn1, bf(w_ffn1_gate[0]), bf(w_ffn1_up[0]), bf(w_ffn1_down[0]), gf)

    bp, tp, _ = x_prompt.shape
    x1p, (win,) = _ffn(x_prompt.reshape(bp * tp, d), *ffn1, final_norm=False, tm=prompt_tiles["tm_ffn"],
                       tf=prompt_tiles["tf"], cast=(w_in[0],))
    n_conv = 3 * dc
    w = dict(
        norm_mix=norm_mix, **{"in": win},
        in_groups=dict(conv=0, hgrn=n_conv // LANE_GROUP, gate=(n_conv + 4 * dh) // LANE_GROUP),
        conv_w=conv_w[0], lb_logits=hgrn_lb_logits, hgrn_norm=hgrn_norm, norm_ffn2=norm_ffn2, norm_final=gf,
    )
    later = (w_br_conv[0], w_br_hgrn[0], w_out[0], w_ffn2_gate[0], w_ffn2_up[0], w_ffn2_down[0])
    zero_conv = jnp.zeros((bp,) + cache_conv.shape[2:], F32)
    zero_hgrn = jnp.zeros((bp,) + state_hgrn.shape[2:], F32)
    yp, cp, sp, later = _mix_and_ffn2(x1p, zero_conv, zero_hgrn, w, later, batch=bp, **prompt_tiles)

    bs, ts, _ = x_sample.shape
    x1s, _ = _ffn(x_sample.reshape(bs * ts, d), *ffn1, final_norm=False, tm=sample_tiles["tm_ffn"], tf=sample_tiles["tf"])
    ys, cs, ss, _ = _mix_and_ffn2(x1s, cache_conv[0], state_hgrn[0], w, later, batch=bs, **sample_tiles)
    return yp.reshape(bp, tp, d), ys.reshape(bs, ts, d), cp[None], sp[None], cs[None], ss[None]


def kernel(x_prompt, x_sample, cache_conv, state_hgrn, norm_ffn1, w_ffn1_gate, w_ffn1_up, w_ffn1_down, norm_mix, w_in, conv_w, hgrn_lb_logits, hgrn_norm, w_br_conv, w_br_hgrn, w_out, norm_ffn2, w_ffn2_gate, w_ffn2_up, w_ffn2_down, norm_final):
    return _forward(
        x_prompt, x_sample, cache_conv, state_hgrn, norm_ffn1, w_ffn1_gate, w_ffn1_up, w_ffn1_down, norm_mix, w_in,
        conv_w, hgrn_lb_logits, hgrn_norm, w_br_conv, w_br_hgrn, w_out, norm_ffn2, w_ffn2_gate, w_ffn2_up, w_ffn2_down,
        norm_final,
        prompt_tiles=dict(tm_ffn=1024, tf=512, tm_proj=1024, tm_merge=256, tc_conv=512, seqs_hgrn=1, tc_hgrn=4096,
                          group=128),
        sample_tiles=dict(tm_ffn=1024, tf=512, tm_proj=1024, tm_merge=256, tc_conv=64, seqs_hgrn=16, tc_hgrn=64,
                          group=64))
```

```python
import functools

import numpy as np
import jax
import jax.numpy as jnp
from jax import lax
from jax.experimental import pallas as pl
from jax.experimental.pallas import tpu as pltpu

NORM_EPS = 1e-6
INV_LN2 = 1.4426950408889634
CONV_WIDTH = 3
HGRN_BLOCK = 16
HEAD_DIM = 128
HEADS_PER_STEP = 2
LANE_GROUP = HEADS_PER_STEP * HEAD_DIM
VMEM_LIMIT_BYTES = 56 * 1024 * 1024

F32 = jnp.float32
BF16 = jnp.bfloat16


def _params(semantics):
    return pltpu.CompilerParams(dimension_semantics=semantics, vmem_limit_bytes=VMEM_LIMIT_BYTES)


def _rms(x, gain):
    r = lax.rsqrt(jnp.mean(x * x, axis=-1, keepdims=True) + NORM_EPS)
    return x * r * gain


def _dot(a, b):
    return jnp.dot(a, b, preferred_element_type=F32)


def _dot_nt(a, b):
    return lax.dot_general(a, b, (((1,), (1,)), ((), ())), preferred_element_type=F32)


def _dot_tn(a, b):
    return lax.dot_general(a, b, (((0,), (0,)), ((), ())), preferred_element_type=F32)


CAST_LANE_MULTIPLE = 128
CAST_ROW_MULTIPLE = 16


def _cast_streams(arrays, n_steps, step_of):
    in_specs, out_specs, shapes = [], [], []
    for a in arrays:
        _, rows, cols = a.shape
        cuts = [(nr, nc) for nr in range(1, rows // CAST_ROW_MULTIPLE + 1) if rows % (nr * CAST_ROW_MULTIPLE) == 0
                for nc in range(1, cols // CAST_LANE_MULTIPLE + 1) if cols % (nc * CAST_LANE_MULTIPLE) == 0
                if nr * nc <= n_steps]
        nr, nc = max(cuts, key=lambda c: (c[0] * c[1], c[0]))

        def block(*ids, nr=nr, nc=nc):
            s = jnp.minimum(step_of(*ids), nr * nc - 1)
            return s // nc, s % nc

        in_specs.append(pl.BlockSpec((None, rows // nr, cols // nc), lambda *ids, block=block: (0,) + block(*ids)))
        out_specs.append(pl.BlockSpec((rows // nr, cols // nc), block))
        shapes.append(jax.ShapeDtypeStruct((rows, cols), BF16))
    return in_specs, out_specs, shapes


def _cast_blocks(in_refs, out_refs):
    for src_ref, dst_ref in zip(in_refs, out_refs):
        dst_ref[...] = src_ref[...].astype(BF16)


def _ffn_kernel(*refs, final_norm, n_cast):
    x_ref, g_ref, wg_ref, wu_ref, wd_ref, gf_ref = refs[:6]
    cast_in = refs[6:6 + n_cast]
    o_ref = refs[6 + n_cast]
    cast_out = refs[7 + n_cast:7 + 2 * n_cast]
    h_ref = refs[7 + 2 * n_cast]
    f = pl.program_id(1)

    @pl.when(f == 0)
    def _():
        x = x_ref[...]
        h_ref[...] = _rms(x, g_ref[...]).astype(BF16)
        o_ref[...] = x

    h = h_ref[...]
    g = _dot(h, wg_ref[...])
    u = _dot(h, wu_ref[...])
    a = (g * jax.nn.sigmoid(g) * u * 0.5).astype(BF16)
    o_ref[...] += _dot(a, wd_ref[...])

    _cast_blocks(cast_in, cast_out)

    if final_norm:
        @pl.when(f == pl.num_programs(1) - 1)
        def _():
            o_ref[...] = _rms(o_ref[...], gf_ref[...])


def _ffn(x, gain, wg, wu, wd, final_gain, *, final_norm, tm, tf, cast=()):
    m, d = x.shape
    dff = wg.shape[1]
    n_i, n_f = m // tm, dff // tf
    cast_in_specs, cast_out_specs, cast_shapes = _cast_streams(cast, n_i * n_f, lambda i, f: i * n_f + f)
    outs = pl.pallas_call(
        functools.partial(_ffn_kernel, final_norm=final_norm, n_cast=len(cast)),
        grid=(n_i, n_f),
        in_specs=[
            pl.BlockSpec((tm, d), lambda i, f: (i, 0)),
            pl.BlockSpec((1, d), lambda i, f: (0, 0)),
            pl.BlockSpec((d, tf), lambda i, f: (0, f)),
            pl.BlockSpec((d, tf), lambda i, f: (0, f)),
            pl.BlockSpec((tf, d), lambda i, f: (f, 0)),
            pl.BlockSpec((1, d), lambda i, f: (0, 0)),
        ] + cast_in_specs,
        out_specs=[pl.BlockSpec((tm, d), lambda i, f: (i, 0))] + cast_out_specs,
        out_shape=[jax.ShapeDtypeStruct((m, d), F32)] + cast_shapes,
        scratch_shapes=[pltpu.VMEM((tm, d), BF16)],
        compiler_params=_params(("arbitrary", "arbitrary")),
        name="ffn_final" if final_norm else "ffn",
    )(x, gain, wg, wu, wd, final_gain, *cast)
    return outs[0], outs[1:]


def _proj_kernel(x_ref, g_ref, w_ref, o_ref, h_ref, *, group_width):
    @pl.when(pl.program_id(1) == 0)
    def _():
        h_ref[...] = _rms(x_ref[...], g_ref[...]).astype(BF16)

    res = _dot(h_ref[...], w_ref[...]).astype(o_ref.dtype)
    for k in range(o_ref.shape[0]):
        o_ref[k] = res[:, k * group_width:(k + 1) * group_width]


def _proj(x, gain, w, *, group_width, tm, tn):
    m, d = x.shape
    n = w.shape[1]
    gpb = tn // group_width
    return pl.pallas_call(
        functools.partial(_proj_kernel, group_width=group_width),
        grid=(m // tm, n // tn),
        in_specs=[
            pl.BlockSpec((tm, d), lambda i, j: (i, 0)),
            pl.BlockSpec((1, d), lambda i, j: (0, 0)),
            pl.BlockSpec((d, tn), lambda i, j: (0, j)),
        ],
        out_specs=pl.BlockSpec((gpb, tm, group_width), lambda i, j: (j, i, 0)),
        out_shape=jax.ShapeDtypeStruct((n // group_width, m, group_width), BF16),
        scratch_shapes=[pltpu.VMEM((tm, d), BF16)],
        compiler_params=_params(("arbitrary", "arbitrary")),
        name="in_proj",
    )(x, gain, w)


def _conv_kernel(b_ref, c_ref, v_ref, c0_ref, w_ref, y_ref, tail_ref, ubuf, *, tc):
    @pl.when(pl.program_id(1) == 0)
    def _():
        tail_ref[...] = c0_ref[...]

    ng = y_ref.shape[0]
    wide = lambda ref: jnp.concatenate([ref[g] for g in range(ng)], axis=1).astype(F32)
    pad = 8
    u = wide(c_ref) * wide(v_ref)
    ubuf[pad - (CONV_WIDTH - 1):pad, :] = tail_ref[0]
    ubuf[pad:pad + tc, :] = u
    w = w_ref[...]
    acc = w[CONV_WIDTH - 1:CONV_WIDTH] * u
    for j in range(CONV_WIDTH - 1):
        shift = CONV_WIDTH - 1 - j
        acc = acc + w[j:j + 1] * ubuf[pad - shift:pad - shift + tc, :]
    y = (wide(b_ref) * acc).astype(y_ref.dtype)
    for g in range(ng):
        y_ref[g] = y[:, g * LANE_GROUP:(g + 1) * LANE_GROUP]
    tail_ref[0] = u[tc - (CONV_WIDTH - 1):, :]


def _conv(p, conv0, conv_w, *, first_group, batch, tc):
    _, m, gw = p.shape
    dc = conv0.shape[-1]
    ng = dc // gw
    assert first_group % ng == 0
    t = m // batch
    nc = t // tc
    row = lambda b, c: b * nc + c
    part = lambda k: pl.BlockSpec((ng, tc, gw), lambda b, c: (first_group // ng + k, row(b, c), 0))
    return pl.pallas_call(
        functools.partial(_conv_kernel, tc=tc),
        grid=(batch, nc),
        in_specs=[
            part(0), part(1), part(2),
            pl.BlockSpec((1, CONV_WIDTH - 1, dc), lambda b, c: (b, 0, 0)),
            pl.BlockSpec((CONV_WIDTH, dc), lambda b, c: (0, 0)),
        ],
        out_specs=[
            pl.BlockSpec((ng, tc, gw), lambda b, c: (0, row(b, c), 0)),
            pl.BlockSpec((1, CONV_WIDTH - 1, dc), lambda b, c: (b, 0, 0)),
        ],
        out_shape=[
            jax.ShapeDtypeStruct((ng, m, gw), BF16),
            jax.ShapeDtypeStruct((batch, CONV_WIDTH - 1, dc), F32),
        ],
        scratch_shapes=[pltpu.VMEM((tc + 8, dc), F32)],
        compiler_params=_params(("arbitrary", "arbitrary")),
        name="short_conv",
    )(p, p, p, conv0, conv_w)


def _hgrn_tables(group, width):
    nb = group // HGRN_BLOCK
    halves = []
    h = HGRN_BLOCK
    while h < group:
        halves.append(h)
        h *= 2
    assert halves, "group must span at least two blocks"
    tok = np.arange(group)

    def span(b0, b1):
        return ((tok >= b0 * HGRN_BLOCK) & (tok < b1 * HGRN_BLOCK)).astype(np.float32)

    zero = np.zeros(group, np.float32)
    rows, valid, index = [], [], {}
    for h in halves:
        hb = h // HGRN_BLOCK
        krows, kval, qrows, qval = [], [], [], []
        for b in range(nb):
            bmid = (b // (2 * hb)) * 2 * hb + hb
            first = b < bmid
            krows.append(span(b, bmid) if first else zero)
            kval.append(1.0 if first else 0.0)
            qrows.append(zero if first else span(bmid, b))
            qval.append(0.0 if first else 1.0)
        index[("k", h)] = len(rows)
        rows += krows
        valid += kval
        if hb > 1:
            index[("q", h)] = len(rows)
            rows += qrows
            valid += qval
    index["q_in"] = len(rows)
    rows += [span(0, b) for b in range(nb)]
    valid += [1.0] * nb
    index["k_out"] = len(rows)
    rows += [span(b, nb) for b in range(nb)]
    valid += [1.0] * nb
    index["decay"] = len(rows)
    rows.append(span(0, nb))
    valid.append(1.0)
    nx = -(-len(rows) // 16) * 16
    while len(rows) < nx:
        rows.append(zero)
        valid.append(0.0)

    local = ((tok[:, None] // HGRN_BLOCK == tok[None, :] // HGRN_BLOCK) & (tok[None, :] <= tok[:, None]))
    lhs = np.concatenate([local.astype(np.float32), np.stack(rows)], axis=0)
    lhs = np.concatenate([lhs, lhs], axis=1)
    valid = np.broadcast_to(np.asarray(valid, np.float32)[:, None], (nx, width)).copy()

    rb, cb = tok[:, None] // HGRN_BLOCK, tok[None, :] // HGRN_BLOCK
    cls = np.full((group, group), len(halves), np.int32)
    for li, h in enumerate(halves[:-1]):
        hb = h // HGRN_BLOCK
        own = (rb // (2 * hb) == cb // (2 * hb)) & ((rb // hb) % 2 == 1) & ((cb // hb) % 2 == 0)
        cls[own] = li + 1
    cls[local] = 0
    return lhs, valid, cls, index, halves


def _hgrn_kernel(*refs, layer, group, seqs, seq_rows, index, halves, n_cast):
    q_ref, f_ref, i_ref, og_ref, lbl_ref, nrm_ref, s0_ref, lhs_ref, valid_ref, cls_ref = refs[:10]
    cast_in = refs[10:10 + n_cast]
    y_ref, s_ref = refs[10 + n_cast:12 + n_cast]
    cast_out = refs[12 + n_cast:12 + 2 * n_cast]
    st_ref, fc_ref = refs[12 + 2 * n_cast:]
    chunk = pl.program_id(2)
    width = HEADS_PER_STEP * HEAD_DIM
    nb = group // HGRN_BLOCK
    _cast_blocks(cast_in, cast_out)

    @pl.when(chunk == 0)
    def _():
        for j in range(seqs):
            for hh in range(HEADS_PER_STEP):
                st_ref[j, hh] = s0_ref[j, hh].T

    logits = lbl_ref[...]
    e = jnp.exp(logits - jnp.max(logits, axis=0, keepdims=True))
    lb = jnp.sum(e[:layer + 1], axis=0, keepdims=True) / jnp.sum(e, axis=0, keepdims=True)
    lb1 = 1.0 - lb
    gain = nrm_ref[...]
    lhs = lhs_ref[...]
    valid = valid_ref[...]
    cls = cls_ref[...]
    masks = [cls == ci for ci in range(len(halves))]

    n_items = seqs * (seq_rows // group)

    def rows_of(item):
        j, g = divmod(item, seq_rows // group)
        return j, slice(j * seq_rows + g * group, j * seq_rows + (g + 1) * group)

    def front(item):
        j, rs = rows_of(item)
        q = q_ref[0, rs, :].astype(F32)
        fz = f_ref[0, rs, :].astype(F32)
        v = i_ref[0, rs, :]
        og = og_ref[0, rs, :].astype(F32)

        f = lb + lb1 * jax.nn.sigmoid(fz)
        lg = jnp.log(f) * INV_LN2
        k = 1.0 - f
        qf = q * jax.nn.sigmoid(q)

        top = pltpu.bitcast(pltpu.bitcast(lg, jnp.uint32) & jnp.uint32(0xFFFF0000), F32)
        pieces = jnp.concatenate([top.astype(BF16), (lg - top).astype(BF16)], axis=0)
        sums = _dot(lhs, pieces)
        e_loc = sums[:group]
        fc_ref[item] = jnp.exp2(sums[group:]) * valid

        def expand(name):
            base = index[name]
            tiles = []
            for b in range(nb):
                tile = jnp.broadcast_to(fc_ref[item, pl.ds(base + b, 1), :], (8, width))
                tiles.append(jnp.concatenate([tile] * (HGRN_BLOCK // 8), axis=0).astype(BF16))
            return jnp.concatenate(tiles, axis=0)

        qd_b = (qf * jnp.exp2(e_loc)).astype(BF16)
        kd_b = (k * jnp.exp2(-e_loc)).astype(BF16)
        q_lv = [qd_b if h == HGRN_BLOCK else qd_b * expand(("q", h)) for h in halves]
        k_lv = [kd_b * expand(("k", h)) for h in halves]
        q_in = qd_b * expand("q_in")
        k_out = kd_b * expand("k_out")
        sog = (og * jax.nn.sigmoid(og)) * gain
        return qd_b, kd_b, q_lv, k_lv, q_in, k_out, v, sog

    def mid(ops):
        qd_b, kd_b, q_lv, k_lv, q_in, k_out, v, sog = ops
        a_heads = []
        for hh in range(HEADS_PER_STEP):
            sl = slice(hh * HEAD_DIM, (hh + 1) * HEAD_DIM)
            if group % 128 == 0:
                both = _dot_nt(qd_b[:, sl], jnp.concatenate([kd_b[:, sl], k_lv[0][:, sl]], axis=0))
                parts = [both[:, :group], both[:, group:]]
            else:
                parts = [_dot_nt(qd_b[:, sl], kd_b[:, sl]), _dot_nt(qd_b[:, sl], k_lv[0][:, sl])]
            parts += [_dot_nt(ql[:, sl], kl[:, sl]) for ql, kl in zip(q_lv[1:], k_lv[1:])]
            a = parts[-1]
            for ci in reversed(range(len(halves))):
                a = jnp.where(masks[ci], parts[ci], a)
            a_heads.append(a.astype(BF16))
        return a_heads, q_in, k_out, v, sog

    def back(item, ops):
        a_heads, q_in, k_out, v, sog = ops
        j, rs = rows_of(item)
        decay = fc_ref[item, pl.ds(index["decay"], 1), :]
        outs = []
        for hh in range(HEADS_PER_STEP):
            sl = slice(hh * HEAD_DIM, (hh + 1) * HEAD_DIM)
            st = st_ref[j, hh]
            o = _dot(a_heads[hh], v[:, sl]) + _dot_nt(q_in[:, sl], st.astype(BF16))
            st_ref[j, hh] = st * decay[:, sl] + _dot_tn(v[:, sl], k_out[:, sl])
            outs.append(o * lax.rsqrt(jnp.mean(o * o, axis=-1, keepdims=True) + NORM_EPS))
        y_ref[0, rs, :] = (jnp.concatenate(outs, axis=1) * sog).astype(y_ref.dtype)

    fronts, mids = {}, {}
    for t in range(n_items + 2):
        if t < n_items:
            fronts[t] = front(t)
        if 0 <= t - 1 < n_items:
            mids[t - 1] = mid(fronts.pop(t - 1))
        if 0 <= t - 2 < n_items:
            back(t - 2, mids.pop(t - 2))

    @pl.when(chunk == pl.num_programs(2) - 1)
    def _():
        for j in range(seqs):
            for hh in range(HEADS_PER_STEP):
                s_ref[j, hh] = st_ref[j, hh].T


def _hgrn(p, lb_logits, hgrn_norm, s0, *, first_group, layer, batch, seqs, tc, group, cast=()):
    _, m, width = p.shape
    npairs = s0.shape[1] // HEADS_PER_STEP
    t = m // batch
    nc = t // tc
    assert seqs == 1 or nc == 1
    rows = seqs * tc
    lhs, valid, cls, index, halves = _hgrn_tables(group, width)
    row = lambda b, c: b * nc + c
    part = lambda k: pl.BlockSpec((1, rows, width), lambda b, hp, c: (first_group + k * npairs + hp, row(b, c), 0))
    state_spec = pl.BlockSpec((seqs, HEADS_PER_STEP, HEAD_DIM, HEAD_DIM), lambda b, hp, c: (b, hp, 0, 0))
    const = lambda a: pl.BlockSpec(a.shape, lambda b, hp, c: (0, 0))
    n_items = seqs * (tc // group)
    grid = (batch // seqs, npairs, nc)
    cast_in_specs, cast_out_specs, cast_shapes = _cast_streams(
        cast, grid[0] * grid[1] * grid[2], lambda b, hp, c: (b * npairs + hp) * nc + c)
    outs = pl.pallas_call(
        functools.partial(_hgrn_kernel, layer=layer, group=group, seqs=seqs, seq_rows=tc, index=index,
                          halves=halves, n_cast=len(cast)),
        grid=grid,
        in_specs=[
            part(0), part(1), part(2), part(3),
            pl.BlockSpec((lb_logits.shape[0], width), lambda b, hp, c: (0, hp)),
            pl.BlockSpec((1, width), lambda b, hp, c: (0, hp)),
            state_spec, const(lhs), const(valid), const(cls),
        ] + cast_in_specs,
        out_specs=[
            pl.BlockSpec((1, rows, width), lambda b, hp, c: (hp, row(b, c), 0)),
            state_spec,
        ] + cast_out_specs,
        out_shape=[
            jax.ShapeDtypeStruct((npairs, m, width), BF16),
            jax.ShapeDtypeStruct(s0.shape, F32),
        ] + cast_shapes,
        scratch_shapes=[
            pltpu.VMEM((seqs, HEADS_PER_STEP, HEAD_DIM, HEAD_DIM), F32),
            pltpu.VMEM((n_items, valid.shape[0], width), F32),
        ],
        compiler_params=_params(("arbitrary", "arbitrary", "arbitrary")),
        name="hgrn2",
    )(p, p, p, p, lb_logits, hgrn_norm, s0, jnp.asarray(lhs, BF16), jnp.asarray(valid), jnp.asarray(cls), *cast)
    return outs[0], outs[1], outs[2:]


def _merge_kernel(yc_ref, yh_ref, gc0_ref, gc1_ref, gh0_ref, gh1_ref, x_ref, wbc_ref, wbh_ref, wo_ref, o_ref):
    def branch(y_ref, w_ref):
        acc = _dot(y_ref[0], w_ref[0])
        for g in range(1, y_ref.shape[0]):
            acc = acc + _dot(y_ref[g], w_ref[g])
        return acc

    def gate(*halves):
        return jax.nn.sigmoid(jnp.concatenate([r[g] for r in halves for g in range(r.shape[0])], axis=1).astype(F32))

    merged = (gate(gc0_ref, gc1_ref) * branch(yc_ref, wbc_ref)
              + gate(gh0_ref, gh1_ref) * branch(yh_ref, wbh_ref)).astype(BF16)
    o_ref[...] = x_ref[...] + _dot(merged, wo_ref[...])


def _merge(yc, yh, p, x, wbc, wbh, wo, *, gate_group, tm):
    m, d = x.shape
    gw = p.shape[-1]
    nhalf = d // gw // 2
    assert gate_group % nhalf == 0
    const = lambda shape: pl.BlockSpec(shape, lambda i: (0,) * len(shape), pipeline_mode=pl.Buffered(1))
    rows = lambda a: pl.BlockSpec((a.shape[0], tm, gw), lambda i: (0, i, 0))
    gate = lambda k: pl.BlockSpec((nhalf, tm, gw), lambda i: (gate_group // nhalf + k, i, 0))
    return pl.pallas_call(
        _merge_kernel,
        grid=(m // tm,),
        in_specs=[
            rows(yc), rows(yh), gate(0), gate(1), gate(2), gate(3),
            pl.BlockSpec((tm, d), lambda i: (i, 0)),
            const(wbc.shape), const(wbh.shape), const(wo.shape),
        ],
        out_specs=pl.BlockSpec((tm, d), lambda i: (i, 0)),
        out_shape=jax.ShapeDtypeStruct((m, d), F32),
        compiler_params=_params(("arbitrary",)),
        name="merge_out",
    )(yc, yh, p, p, p, p, x, wbc, wbh, wo)


def _mix_and_ffn2(x1, conv0, s0, w, later, *, batch, tm_ffn, tf, tm_proj, tm_merge, tc_conv, seqs_hgrn, tc_hgrn, group):
    d = x1.shape[1]
    groups = w["in_groups"]
    p = _proj(x1, w["norm_mix"], w["in"], group_width=LANE_GROUP, tm=tm_proj, tn=2560)
    yc, conv_new = _conv(p, conv0, w["conv_w"], first_group=groups["conv"], batch=batch, tc=tc_conv)
    pending = [a for a in later if a.dtype != BF16]
    yh, s_new, done = _hgrn(p, w["lb_logits"], w["hgrn_norm"], s0, first_group=groups["hgrn"], layer=0, batch=batch,
                            seqs=seqs_hgrn, tc=tc_hgrn, group=group, cast=pending)
    later = tuple(done) if pending else later
    wbc, wbh, wo, w2g, w2u, w2d = later
    grouped = lambda a: a.reshape(a.shape[0] // LANE_GROUP, LANE_GROUP, d)
    x2 = _merge(yc, yh, p, x1, grouped(wbc), grouped(wbh), wo, gate_group=groups["gate"], tm=tm_merge)
    y, _ = _ffn(x2, w["norm_ffn2"], w2g, w2u, w2d, w["norm_final"], final_norm=True, tm=tm_ffn, tf=tf)
    return y, conv_new, s_new, later


def _forward(x_prompt, x_sample, cache_conv, state_hgrn, norm_ffn1, w_ffn1_gate, w_ffn1_up, w_ffn1_down, norm_mix, w_in,
             conv_w, hgrn_lb_logits, hgrn_norm, w_br_conv, w_br_hgrn, w_out, norm_ffn2, w_ffn2_gate, w_ffn2_up,
             w_ffn2_down, norm_final, *, prompt_tiles, sample_tiles):
    assert w_in.shape[0] == 1, "single-layer trunk"
    heads, dk, dv = state_hgrn.shape[-3:]
    assert dk == HEAD_DIM and dv == HEAD_DIM and heads % HEADS_PER_STEP == 0
    d = x_prompt.shape[-1]
    dc, dh = cache_conv.shape[-1], heads * dk
    bf = lambda a: a.astype(BF16)
    gf = norm_final.reshape(1, d)
    ffn1 = (norm_ffn1, bf(w_ffn1_gate[0]), bf(w_ffn1_up[0]), bf(w_ffn1_down[0]), gf)

    bp, tp, _ = x_prompt.shape
    x1p, (win,) = _ffn(x_prompt.reshape(bp * tp, d), *ffn1, final_norm=False, tm=prompt_tiles["tm_ffn"],
                       tf=prompt_tiles["tf"], cast=(w_in,))
    n_conv = 3 * dc
    w = dict(
        norm_mix=norm_mix, **{"in": win},
        in_groups=dict(conv=0, hgrn=n_conv // LANE_GROUP, gate=(n_conv + 4 * dh) // LANE_GROUP),
        conv_w=conv_w[0], lb_logits=hgrn_lb_logits, hgrn_norm=hgrn_norm, norm_ffn2=norm_ffn2, norm_final=gf,
    )
    later = (w_br_conv, w_br_hgrn, w_out, w_ffn2_gate, w_ffn2_up, w_ffn2_down)
    zero_conv = jnp.zeros((bp,) + cache_conv.shape[2:], F32)
    zero_hgrn = jnp.zeros((bp,) + state_hgrn.shape[2:], F32)
    yp, cp, sp, later = _mix_and_ffn2(x1p, zero_conv, zero_hgrn, w, later, batch=bp, **prompt_tiles)

    bs, ts, _ = x_sample.shape
    x1s, _ = _ffn(x_sample.reshape(bs * ts, d), *ffn1, final_norm=False, tm=sample_tiles["tm_ffn"], tf=sample_tiles["tf"])
    ys, cs, ss, _ = _mix_and_ffn2(x1s, cache_conv[0], state_hgrn[0], w, later, batch=bs, **sample_tiles)
    return yp.reshape(bp, tp, d), ys.reshape(bs, ts, d), cp[None], sp[None], cs[None], ss[None]


def kernel(x_prompt, x_sample, cache_conv, state_hgrn, norm_ffn1, w_ffn1_gate, w_ffn1_up, w_ffn1_down, norm_mix, w_in, conv_w, hgrn_lb_logits, hgrn_norm, w_br_conv, w_br_hgrn, w_out, norm_ffn2, w_ffn2_gate, w_ffn2_up, w_ffn2_down, norm_final):
    return _forward(
        x_prompt, x_sample, cache_conv, state_hgrn, norm_ffn1, w_ffn1_gate, w_ffn1_up, w_ffn1_down, norm_mix, w_in,
        conv_w, hgrn_lb_logits, hgrn_norm, w_br_conv, w_br_hgrn, w_out, norm_ffn2, w_ffn2_gate, w_ffn2_up, w_ffn2_down,
        norm_final,
        prompt_tiles=dict(tm_ffn=1024, tf=512, tm_proj=1024, tm_merge=256, tc_conv=512, seqs_hgrn=1, tc_hgrn=4096,
                          group=128),
        sample_tiles=dict(tm_ffn=1024, tf=512, tm_proj=1024, tm_merge=256, tc_conv=64, seqs_hgrn=16, tc_hgrn=64,
                          group=64))
```

```python
import functools

import numpy as np
import jax
import jax.numpy as jnp
from jax import lax
from jax.experimental import pallas as pl
from jax.experimental.pallas import tpu as pltpu

NORM_EPS = 1e-6
INV_LN2 = 1.4426950408889634
CONV_WIDTH = 3
HGRN_BLOCK = 16
HEAD_DIM = 128
HEADS_PER_STEP = 2
LANE_GROUP = HEADS_PER_STEP * HEAD_DIM
VMEM_LIMIT_BYTES = 56 * 1024 * 1024

F32 = jnp.float32
BF16 = jnp.bfloat16


def _params(semantics):
    return pltpu.CompilerParams(dimension_semantics=semantics, vmem_limit_bytes=VMEM_LIMIT_BYTES)


def _rms(x, gain):
    r = lax.rsqrt(jnp.mean(x * x, axis=-1, keepdims=True) + NORM_EPS)
    return x * r * gain


def _dot(a, b):
    return jnp.dot(a, b, preferred_element_type=F32)


def _dot_nt(a, b):
    return lax.dot_general(a, b, (((1,), (1,)), ((), ())), preferred_element_type=F32)


def _dot_tn(a, b):
    return lax.dot_general(a, b, (((0,), (0,)), ((), ())), preferred_element_type=F32)


CAST_LANE_MULTIPLE = 128
CAST_ROW_MULTIPLE = 16


def _cast_streams(arrays, n_steps, step_of):
    in_specs, out_specs, shapes = [], [], []
    for a in arrays:
        _, rows, cols = a.shape
        cuts = [(nr, nc) for nr in range(1, rows // CAST_ROW_MULTIPLE + 1) if rows % (nr * CAST_ROW_MULTIPLE) == 0
                for nc in range(1, cols // CAST_LANE_MULTIPLE + 1) if cols % (nc * CAST_LANE_MULTIPLE) == 0
                if nr * nc <= n_steps]
        nr, nc = max(cuts, key=lambda c: (c[0] * c[1], c[0]))

        def block(*ids, nr=nr, nc=nc):
            s = jnp.minimum(step_of(*ids), nr * nc - 1)
            return s // nc, s % nc

        in_specs.append(pl.BlockSpec((None, rows // nr, cols // nc), lambda *ids, block=block: (0,) + block(*ids)))
        out_specs.append(pl.BlockSpec((rows // nr, cols // nc), block))
        shapes.append(jax.ShapeDtypeStruct((rows, cols), BF16))
    return in_specs, out_specs, shapes


def _cast_blocks(in_refs, out_refs):
    for src_ref, dst_ref in zip(in_refs, out_refs):
        dst_ref[...] = src_ref[...].astype(BF16)


def _ffn_kernel(*refs, final_norm, n_cast):
    x_ref, g_ref, wg_ref, wu_ref, wd_ref, gf_ref = refs[:6]
    cast_in = refs[6:6 + n_cast]
    o_ref = refs[6 + n_cast]
    cast_out = refs[7 + n_cast:7 + 2 * n_cast]
    h_ref = refs[7 + 2 * n_cast]
    f = pl.program_id(1)

    @pl.when(f == 0)
    def _():
        x = x_ref[...]
        h_ref[...] = _rms(x, g_ref[...]).astype(BF16)
        o_ref[...] = x

    h = h_ref[...]
    g = _dot(h, wg_ref[...])
    u = _dot(h, wu_ref[...])
    a = (g * jax.nn.sigmoid(g) * u * 0.5).astype(BF16)
    o_ref[...] += _dot(a, wd_ref[...])

    _cast_blocks(cast_in, cast_out)

    if final_norm:
        @pl.when(f == pl.num_programs(1) - 1)
        def _():
            o_ref[...] = _rms(o_ref[...], gf_ref[...])


def _ffn(x, gain, wg, wu, wd, final_gain, *, final_norm, tm, tf, cast=()):
    m, d = x.shape
    dff = wg.shape[1]
    n_i, n_f = m // tm, dff // tf
    cast_in_specs, cast_out_specs, cast_shapes = _cast_streams(cast, n_i * n_f, lambda i, f: i * n_f + f)
    outs = pl.pallas_call(
        functools.partial(_ffn_kernel, final_norm=final_norm, n_cast=len(cast)),
        grid=(n_i, n_f),
        in_specs=[
            pl.BlockSpec((tm, d), lambda i, f: (i, 0)),
            pl.BlockSpec((1, d), lambda i, f: (0, 0)),
            pl.BlockSpec((d, tf), lambda i, f: (0, f)),
            pl.BlockSpec((d, tf), lambda i, f: (0, f)),
            pl.BlockSpec((tf, d), lambda i, f: (f, 0)),
            pl.BlockSpec((1, d), lambda i, f: (0, 0)),
        ] + cast_in_specs,
        out_specs=[pl.BlockSpec((tm, d), lambda i, f: (i, 0))] + cast_out_specs,
        out_shape=[jax.ShapeDtypeStruct((m, d), F32)] + cast_shapes,
        scratch_shapes=[pltpu.VMEM((tm, d), BF16)],
        compiler_params=_params(("arbitrary", "arbitrary")),
        name="ffn_final" if final_norm else "ffn",
    )(x, gain, wg, wu, wd, final_gain, *cast)
    return outs[0], outs[1:]


def _proj_kernel(x_ref, g_ref, w_ref, o_ref, h_ref, *, group_width):
    @pl.when(pl.program_id(1) == 0)
    def _():
        h_ref[...] = _rms(x_ref[...], g_ref[...]).astype(BF16)

    res = _dot(h_ref[...], w_ref[...]).astype(o_ref.dtype)
    for k in range(o_ref.shape[0]):
        o_ref[k] = res[:, k * group_width:(k + 1) * group_width]


def _proj(x, gain, w, *, group_width, tm, tn):
    m, d = x.shape
    n = w.shape[1]
    gpb = tn // group_width
    return pl.pallas_call(
        functools.partial(_proj_kernel, group_width=group_width),
        grid=(m // tm, n // tn),
        in_specs=[
            pl.BlockSpec((tm, d), lambda i, j: (i, 0)),
            pl.BlockSpec((1, d), lambda i, j: (0, 0)),
            pl.BlockSpec((d, tn), lambda i, j: (0, j)),
        ],
        out_specs=pl.BlockSpec((gpb, tm, group_width), lambda i, j: (j, i, 0)),
        out_shape=jax.ShapeDtypeStruct((n // group_width, m, group_width), BF16),
        scratch_shapes=[pltpu.VMEM((tm, d), BF16)],
        compiler_params=_params(("arbitrary", "arbitrary")),
        name="in_proj",
    )(x, gain, w)


def _conv_rows(b, c, v, prev, w, ubuf):
    t = b.shape[0]
    pad = 8
    u = c * v
    ubuf[pad - (CONV_WIDTH - 1):pad, :] = prev
    ubuf[pad:pad + t, :] = u
    acc = w[CONV_WIDTH - 1:CONV_WIDTH] * u
    for j in range(CONV_WIDTH - 1):
        shift = CONV_WIDTH - 1 - j
        acc = acc + w[j:j + 1] * ubuf[pad - shift:pad - shift + t, :]
    return b * acc, u[t - (CONV_WIDTH - 1):, :]


def _hgrn_tables(group, width):
    nb = group // HGRN_BLOCK
    halves = []
    h = HGRN_BLOCK
    while h < group:
        halves.append(h)
        h *= 2
    assert halves, "group must span at least two blocks"
    tok = np.arange(group)

    def span(b0, b1):
        return ((tok >= b0 * HGRN_BLOCK) & (tok < b1 * HGRN_BLOCK)).astype(np.float32)

    zero = np.zeros(group, np.float32)
    rows, valid, index = [], [], {}
    for h in halves:
        hb = h // HGRN_BLOCK
        krows, kval, qrows, qval = [], [], [], []
        for b in range(nb):
            bmid = (b // (2 * hb)) * 2 * hb + hb
            first = b < bmid
            krows.append(span(b, bmid) if first else zero)
            kval.append(1.0 if first else 0.0)
            qrows.append(zero if first else span(bmid, b))
            qval.append(0.0 if first else 1.0)
        index[("k", h)] = len(rows)
        rows += krows
        valid += kval
        if hb > 1:
            index[("q", h)] = len(rows)
            rows += qrows
            valid += qval
    index["q_in"] = len(rows)
    rows += [span(0, b) for b in range(nb)]
    valid += [1.0] * nb
    index["k_out"] = len(rows)
    rows += [span(b, nb) for b in range(nb)]
    valid += [1.0] * nb
    index["decay"] = len(rows)
    rows.append(span(0, nb))
    valid.append(1.0)
    nx = -(-len(rows) // 16) * 16
    while len(rows) < nx:
        rows.append(zero)
        valid.append(0.0)

    local = ((tok[:, None] // HGRN_BLOCK == tok[None, :] // HGRN_BLOCK) & (tok[None, :] <= tok[:, None]))
    lhs = np.concatenate([local.astype(np.float32), np.stack(rows)], axis=0)
    lhs = np.concatenate([lhs, lhs], axis=1)
    valid = np.broadcast_to(np.asarray(valid, np.float32)[:, None], (nx, width)).copy()

    rb, cb = tok[:, None] // HGRN_BLOCK, tok[None, :] // HGRN_BLOCK
    cls = np.full((group, group), len(halves), np.int32)
    for li, h in enumerate(halves[:-1]):
        hb = h // HGRN_BLOCK
        own = (rb // (2 * hb) == cb // (2 * hb)) & ((rb // hb) % 2 == 1) & ((cb // hb) % 2 == 0)
        cls[own] = li + 1
    cls[local] = 0
    return lhs, valid, cls, index, halves


def _hgrn_kernel(*refs, layer, group, seqs, seq_rows, index, halves, n_cast):
    q_ref, f_ref, i_ref, og_ref, lbl_ref, nrm_ref, s0_ref, lhs_ref, valid_ref, cls_ref = refs[:10]
    cast_in = refs[10:10 + n_cast]
    y_ref, s_ref = refs[10 + n_cast:12 + n_cast]
    cast_out = refs[12 + n_cast:12 + 2 * n_cast]
    st_ref, fc_ref = refs[12 + 2 * n_cast:]
    chunk = pl.program_id(2)
    width = HEADS_PER_STEP * HEAD_DIM
    nb = group // HGRN_BLOCK
    _cast_blocks(cast_in, cast_out)

    @pl.when(chunk == 0)
    def _():
        for j in range(seqs):
            for hh in range(HEADS_PER_STEP):
                st_ref[j, hh] = s0_ref[j, hh].T

    logits = lbl_ref[...]
    e = jnp.exp(logits - jnp.max(logits, axis=0, keepdims=True))
    lb = jnp.sum(e[:layer + 1], axis=0, keepdims=True) / jnp.sum(e, axis=0, keepdims=True)
    lb1 = 1.0 - lb
    gain = nrm_ref[...]
    lhs = lhs_ref[...]
    valid = valid_ref[...]
    cls = cls_ref[...]
    masks = [cls == ci for ci in range(len(halves))]

    n_items = seqs * (seq_rows // group)

    def rows_of(item):
        j, g = divmod(item, seq_rows // group)
        return j, slice(j * seq_rows + g * group, j * seq_rows + (g + 1) * group)

    def front(item):
        j, rs = rows_of(item)
        q = q_ref[0, rs, :].astype(F32)
        fz = f_ref[0, rs, :].astype(F32)
        v = i_ref[0, rs, :]
        og = og_ref[0, rs, :].astype(F32)

        f = lb + lb1 * jax.nn.sigmoid(fz)
        lg = jnp.log(f) * INV_LN2
        k = 1.0 - f
        qf = q * jax.nn.sigmoid(q)

        top = pltpu.bitcast(pltpu.bitcast(lg, jnp.uint32) & jnp.uint32(0xFFFF0000), F32)
        pieces = jnp.concatenate([top.astype(BF16), (lg - top).astype(BF16)], axis=0)
        sums = _dot(lhs, pieces)
        e_loc = sums[:group]
        fc_ref[item] = jnp.exp2(sums[group:]) * valid

        def expand(name):
            base = index[name]
            tiles = []
            for b in range(nb):
                tile = jnp.broadcast_to(fc_ref[item, pl.ds(base + b, 1), :], (8, width))
                tiles.append(jnp.concatenate([tile] * (HGRN_BLOCK // 8), axis=0).astype(BF16))
            return jnp.concatenate(tiles, axis=0)

        qd_b = (qf * jnp.exp2(e_loc)).astype(BF16)
        kd_b = (k * jnp.exp2(-e_loc)).astype(BF16)
        q_lv = [qd_b if h == HGRN_BLOCK else qd_b * expand(("q", h)) for h in halves]
        k_lv = [kd_b * expand(("k", h)) for h in halves]
        q_in = qd_b * expand("q_in")
        k_out = kd_b * expand("k_out")
        sog = (og * jax.nn.sigmoid(og)) * gain
        return qd_b, kd_b, q_lv, k_lv, q_in, k_out, v, sog

    def mid(ops):
        qd_b, kd_b, q_lv, k_lv, q_in, k_out, v, sog = ops
        a_heads = []
        for hh in range(HEADS_PER_STEP):
            sl = slice(hh * HEAD_DIM, (hh + 1) * HEAD_DIM)
            if group % 128 == 0:
                both = _dot_nt(qd_b[:, sl], jnp.concatenate([kd_b[:, sl], k_lv[0][:, sl]], axis=0))
                parts = [both[:, :group], both[:, group:]]
            else:
                parts = [_dot_nt(qd_b[:, sl], kd_b[:, sl]), _dot_nt(qd_b[:, sl], k_lv[0][:, sl])]
            parts += [_dot_nt(ql[:, sl], kl[:, sl]) for ql, kl in zip(q_lv[1:], k_lv[1:])]
            a = parts[-1]
            for ci in reversed(range(len(halves))):
                a = jnp.where(masks[ci], parts[ci], a)
            a_heads.append(a.astype(BF16))
        return a_heads, q_in, k_out, v, sog

    def back(item, ops):
        a_heads, q_in, k_out, v, sog = ops
        j, rs = rows_of(item)
        decay = fc_ref[item, pl.ds(index["decay"], 1), :]
        outs = []
        for hh in range(HEADS_PER_STEP):
            sl = slice(hh * HEAD_DIM, (hh + 1) * HEAD_DIM)
            st = st_ref[j, hh]
            o = _dot(a_heads[hh], v[:, sl]) + _dot_nt(q_in[:, sl], st.astype(BF16))
            st_ref[j, hh] = st * decay[:, sl] + _dot_tn(v[:, sl], k_out[:, sl])
            outs.append(o * lax.rsqrt(jnp.mean(o * o, axis=-1, keepdims=True) + NORM_EPS))
        y_ref[0, rs, :] = (jnp.concatenate(outs, axis=1) * sog).astype(y_ref.dtype)

    fronts, mids = {}, {}
    for t in range(n_items + 2):
        if t < n_items:
            fronts[t] = front(t)
        if 0 <= t - 1 < n_items:
            mids[t - 1] = mid(fronts.pop(t - 1))
        if 0 <= t - 2 < n_items:
            back(t - 2, mids.pop(t - 2))

    @pl.when(chunk == pl.num_programs(2) - 1)
    def _():
        for j in range(seqs):
            for hh in range(HEADS_PER_STEP):
                s_ref[j, hh] = st_ref[j, hh].T


def _hgrn(p, lb_logits, hgrn_norm, s0, *, first_group, layer, batch, seqs, tc, group, cast=()):
    _, m, width = p.shape
    npairs = s0.shape[1] // HEADS_PER_STEP
    t = m // batch
    nc = t // tc
    assert seqs == 1 or nc == 1
    rows = seqs * tc
    lhs, valid, cls, index, halves = _hgrn_tables(group, width)
    row = lambda b, c: b * nc + c
    part = lambda k: pl.BlockSpec((1, rows, width), lambda b, hp, c: (first_group + k * npairs + hp, row(b, c), 0))
    state_spec = pl.BlockSpec((seqs, HEADS_PER_STEP, HEAD_DIM, HEAD_DIM), lambda b, hp, c: (b, hp, 0, 0))
    const = lambda a: pl.BlockSpec(a.shape, lambda b, hp, c: (0, 0))
    n_items = seqs * (tc // group)
    grid = (batch // seqs, npairs, nc)
    cast_in_specs, cast_out_specs, cast_shapes = _cast_streams(
        cast, grid[0] * grid[1] * grid[2], lambda b, hp, c: (b * npairs + hp) * nc + c)
    outs = pl.pallas_call(
        functools.partial(_hgrn_kernel, layer=layer, group=group, seqs=seqs, seq_rows=tc, index=index,
                          halves=halves, n_cast=len(cast)),
        grid=grid,
        in_specs=[
            part(0), part(1), part(2), part(3),
            pl.BlockSpec((lb_logits.shape[0], width), lambda b, hp, c: (0, hp)),
            pl.BlockSpec((1, width), lambda b, hp, c: (0, hp)),
            state_spec, const(lhs), const(valid), const(cls),
        ] + cast_in_specs,
        out_specs=[
            pl.BlockSpec((1, rows, width), lambda b, hp, c: (hp, row(b, c), 0)),
            state_spec,
        ] + cast_out_specs,
        out_shape=[
            jax.ShapeDtypeStruct((npairs, m, width), BF16),
            jax.ShapeDtypeStruct(s0.shape, F32),
        ] + cast_shapes,
        scratch_shapes=[
            pltpu.VMEM((seqs, HEADS_PER_STEP, HEAD_DIM, HEAD_DIM), F32),
            pltpu.VMEM((n_items, valid.shape[0], width), F32),
        ],
        compiler_params=_params(("arbitrary", "arbitrary", "arbitrary")),
        name="hgrn2",
    )(p, p, p, p, lb_logits, hgrn_norm, s0, jnp.asarray(lhs, BF16), jnp.asarray(valid), jnp.asarray(cls), *cast)
    return outs[0], outs[1], outs[2:]


def _merge_kernel(b_ref, c_ref, v_ref, c0_ref, cw_ref, yh_ref, gc0_ref, gc1_ref, gh0_ref, gh1_ref, x_ref,
                  wbc_ref, wbh_ref, wo_ref, o_ref, tail_ref, ubuf, *, tiles_per_seq):
    @pl.when(pl.program_id(0) % tiles_per_seq == 0)
    def _():
        tail_ref[...] = c0_ref[...]

    seqs = tail_ref.shape[0]
    t = x_ref.shape[0] // seqs
    ng = b_ref.shape[0]
    wide = lambda ref, rs: jnp.concatenate([ref[g, rs, :] for g in range(ref.shape[0])], axis=1).astype(F32)

    def branch(groups, w_ref):
        acc = _dot(groups[0], w_ref[0])
        for g in range(1, len(groups)):
            acc = acc + _dot(groups[g], w_ref[g])
        return acc

    def gate(*halves):
        return jax.nn.sigmoid(jnp.concatenate([r[g] for r in halves for g in range(r.shape[0])], axis=1).astype(F32))

    bh = branch([yh_ref[g] for g in range(yh_ref.shape[0])], wbh_ref)
    cw = cw_ref[...]
    ys = []
    for s in range(seqs):
        rs = slice(s * t, (s + 1) * t)
        y, tail = _conv_rows(wide(b_ref, rs), wide(c_ref, rs), wide(v_ref, rs), tail_ref[s], cw, ubuf)
        tail_ref[s] = tail
        ys.append(y.astype(BF16))
    yc = jnp.concatenate(ys, axis=0)
    bc = branch([yc[:, g * LANE_GROUP:(g + 1) * LANE_GROUP] for g in range(ng)], wbc_ref)
    merged = (gate(gc0_ref, gc1_ref) * bc + gate(gh0_ref, gh1_ref) * bh).astype(BF16)
    o_ref[...] = x_ref[...] + _dot(merged, wo_ref[...])


def _merge(p, yh, x, conv0, conv_w, wbc, wbh, wo, *, conv_group, gate_group, batch, tm):
    m, d = x.shape
    gw = p.shape[-1]
    dc = conv0.shape[-1]
    ng = dc // gw
    nhalf = d // gw // 2
    assert gate_group % nhalf == 0 and conv_group % ng == 0
    t = m // batch
    seqs_per_tile, tiles_per_seq = max(1, tm // t), max(1, t // tm)
    const = lambda shape: pl.BlockSpec(shape, lambda i: (0,) * len(shape), pipeline_mode=pl.Buffered(1))
    conv_part = lambda k: pl.BlockSpec((ng, tm, gw), lambda i: (conv_group // ng + k, i, 0))
    gate = lambda k: pl.BlockSpec((nhalf, tm, gw), lambda i: (gate_group // nhalf + k, i, 0))
    tail_spec = pl.BlockSpec((seqs_per_tile, CONV_WIDTH - 1, dc), lambda i: (i // tiles_per_seq, 0, 0))
    return pl.pallas_call(
        functools.partial(_merge_kernel, tiles_per_seq=tiles_per_seq),
        grid=(m // tm,),
        in_specs=[
            conv_part(0), conv_part(1), conv_part(2), tail_spec,
            pl.BlockSpec((CONV_WIDTH, dc), lambda i: (0, 0)),
            pl.BlockSpec((yh.shape[0], tm, gw), lambda i: (0, i, 0)),
            gate(0), gate(1), gate(2), gate(3),
            pl.BlockSpec((tm, d), lambda i: (i, 0)),
            const(wbc.shape), const(wbh.shape), const(wo.shape),
        ],
        out_specs=[pl.BlockSpec((tm, d), lambda i: (i, 0)), tail_spec],
        out_shape=[jax.ShapeDtypeStruct((m, d), F32), jax.ShapeDtypeStruct(conv0.shape, F32)],
        scratch_shapes=[pltpu.VMEM((tm // seqs_per_tile + 8, dc), F32)],
        compiler_params=_params(("arbitrary",)),
        name="conv_merge_out",
    )(p, p, p, conv0, conv_w, yh, p, p, p, p, x, wbc, wbh, wo)


def _mix_and_ffn2(x1, conv0, s0, w, later, *, batch, tm_ffn, tf, tm_proj, tm_merge, seqs_hgrn, tc_hgrn, group):
    d = x1.shape[1]
    groups = w["in_groups"]
    p = _proj(x1, w["norm_mix"], w["in"], group_width=LANE_GROUP, tm=tm_proj, tn=2560)
    pending = [a for a in later if a.dtype != BF16]
    yh, s_new, done = _hgrn(p, w["lb_logits"], w["hgrn_norm"], s0, first_group=groups["hgrn"], layer=0, batch=batch,
                            seqs=seqs_hgrn, tc=tc_hgrn, group=group, cast=pending)
    later = tuple(done) if pending else later
    wbc, wbh, wo, w2g, w2u, w2d = later
    grouped = lambda a: a.reshape(a.shape[0] // LANE_GROUP, LANE_GROUP, d)
    x2, conv_new = _merge(p, yh, x1, conv0, w["conv_w"], grouped(wbc), grouped(wbh), wo, conv_group=groups["conv"],
                          gate_group=groups["gate"], batch=batch, tm=tm_merge)
    y, _ = _ffn(x2, w["norm_ffn2"], w2g, w2u, w2d, w["norm_final"], final_norm=True, tm=tm_ffn, tf=tf)
    return y, conv_new, s_new, later


def _forward(x_prompt, x_sample, cache_conv, state_hgrn, norm_ffn1, w_ffn1_gate, w_ffn1_up, w_ffn1_down, norm_mix, w_in,
             conv_w, hgrn_lb_logits, hgrn_norm, w_br_conv, w_br_hgrn, w_out, norm_ffn2, w_ffn2_gate, w_ffn2_up,
             w_ffn2_down, norm_final, *, prompt_tiles, sample_tiles):
    assert w_in.shape[0] == 1, "single-layer trunk"
    heads, dk, dv = state_hgrn.shape[-3:]
    assert dk == HEAD_DIM and dv == HEAD_DIM and heads % HEADS_PER_STEP == 0
    d = x_prompt.shape[-1]
    dc, dh = cache_conv.shape[-1], heads * dk
    bf = lambda a: a.astype(BF16)
    gf = norm_final.reshape(1, d)
    ffn1 = (norm_ffn1, bf(w_ffn1_gate[0]), bf(w_ffn1_up[0]), bf(w_ffn1_down[0]), gf)

    bp, tp, _ = x_prompt.shape
    x1p, (win,) = _ffn(x_prompt.reshape(bp * tp, d), *ffn1, final_norm=False, tm=prompt_tiles["tm_ffn"],
                       tf=prompt_tiles["tf"], cast=(w_in,))
    n_conv = 3 * dc
    w = dict(
        norm_mix=norm_mix, **{"in": win},
        in_groups=dict(conv=0, hgrn=n_conv // LANE_GROUP, gate=(n_conv + 4 * dh) // LANE_GROUP),
        conv_w=conv_w[0], lb_logits=hgrn_lb_logits, hgrn_norm=hgrn_norm, norm_ffn2=norm_ffn2, norm_final=gf,
    )
    later = (w_br_conv, w_br_hgrn, w_out, w_ffn2_gate, w_ffn2_up, w_ffn2_down)
    zero_conv = jnp.zeros((bp,) + cache_conv.shape[2:], F32)
    zero_hgrn = jnp.zeros((bp,) + state_hgrn.shape[2:], F32)
    yp, cp, sp, later = _mix_and_ffn2(x1p, zero_conv, zero_hgrn, w, later, batch=bp, **prompt_tiles)

    bs, ts, _ = x_sample.shape
    x1s, _ = _ffn(x_sample.reshape(bs * ts, d), *ffn1, final_norm=False, tm=sample_tiles["tm_ffn"], tf=sample_tiles["tf"])
    ys, cs, ss, _ = _mix_and_ffn2(x1s, cache_conv[0], state_hgrn[0], w, later, batch=bs, **sample_tiles)
    return yp.reshape(bp, tp, d), ys.reshape(bs, ts, d), cp[None], sp[None], cs[None], ss[None]


def kernel(x_prompt, x_sample, cache_conv, state_hgrn, norm_ffn1, w_ffn1_gate, w_ffn1_up, w_ffn1_down, norm_mix, w_in, conv_w, hgrn_lb_logits, hgrn_norm, w_br_conv, w_br_hgrn, w_out, norm_ffn2, w_ffn2_gate, w_ffn2_up, w_ffn2_down, norm_final):
    return _forward(
        x_prompt, x_sample, cache_conv, state_hgrn, norm_ffn1, w_ffn1_gate, w_ffn1_up, w_ffn1_down, norm_mix, w_in,
        conv_w, hgrn_lb_logits, hgrn_norm, w_br_conv, w_br_hgrn, w_out, norm_ffn2, w_ffn2_gate, w_ffn2_up, w_ffn2_down,
        norm_final,
        prompt_tiles=dict(tm_ffn=1024, tf=512, tm_proj=1024, tm_merge=256, seqs_hgrn=1, tc_hgrn=4096,
                          group=128),
        sample_tiles=dict(tm_ffn=1024, tf=512, tm_proj=1024, tm_merge=256, seqs_hgrn=16, tc_hgrn=64,
                          group=64))
```

```python
import functools

import numpy as np
import jax
import jax.numpy as jnp
from jax import lax
from jax.experimental import pallas as pl
from jax.experimental.pallas import tpu as pltpu

NORM_EPS = 1e-6
INV_LN2 = 1.4426950408889634
CONV_WIDTH = 3
HGRN_BLOCK = 16
HEAD_DIM = 128
HEADS_PER_STEP = 2
LANE_GROUP = HEADS_PER_STEP * HEAD_DIM
LANES = 128
F32_SUBLANES = 8
BF16_SUBLANES = 16
BF16_BITS_OF_F32 = 0xFFFF0000
VMEM_LIMIT_BYTES = 56 * 1024 * 1024

F32 = jnp.float32
BF16 = jnp.bfloat16


def _params(semantics):
    return pltpu.CompilerParams(dimension_semantics=semantics, vmem_limit_bytes=VMEM_LIMIT_BYTES)


def _rms(x, gain):
    r = lax.rsqrt(jnp.mean(x * x, axis=-1, keepdims=True) + NORM_EPS)
    return x * r * gain


def _dot(a, b):
    return jnp.dot(a, b, preferred_element_type=F32)


def _dot_nt(a, b):
    return lax.dot_general(a, b, (((1,), (1,)), ((), ())), preferred_element_type=F32)


def _dot_tn(a, b):
    return lax.dot_general(a, b, (((0,), (0,)), ((), ())), preferred_element_type=F32)


def _cast_streams(arrays, n_steps, step_of):
    in_specs, out_specs, shapes = [], [], []
    for a in arrays:
        _, rows, cols = a.shape
        cuts = [(nr, nc) for nr in range(1, rows // BF16_SUBLANES + 1) if rows % (nr * BF16_SUBLANES) == 0
                for nc in range(1, cols // LANES + 1) if cols % (nc * LANES) == 0
                if nr * nc <= n_steps]
        nr, nc = max(cuts, key=lambda c: (c[0] * c[1], c[0]))

        def block(*ids, nr=nr, nc=nc):
            s = jnp.minimum(step_of(*ids), nr * nc - 1)
            return s // nc, s % nc

        in_specs.append(pl.BlockSpec((None, rows // nr, cols // nc), lambda *ids, block=block: (0,) + block(*ids)))
        out_specs.append(pl.BlockSpec((rows // nr, cols // nc), block))
        shapes.append(jax.ShapeDtypeStruct((rows, cols), BF16))
    return in_specs, out_specs, shapes


def _cast_blocks(in_refs, out_refs):
    for src_ref, dst_ref in zip(in_refs, out_refs):
        dst_ref[...] = src_ref[...].astype(BF16)


def _ffn_kernel(*refs, final_norm, n_cast):
    x_ref, g_ref, wg_ref, wu_ref, wd_ref, gf_ref = refs[:6]
    cast_in = refs[6:6 + n_cast]
    o_ref = refs[6 + n_cast]
    cast_out = refs[7 + n_cast:7 + 2 * n_cast]
    h_ref = refs[7 + 2 * n_cast]
    f = pl.program_id(1)

    @pl.when(f == 0)
    def _():
        x = x_ref[...]
        h_ref[...] = _rms(x, g_ref[...]).astype(BF16)
        o_ref[...] = x

    h = h_ref[...]
    g = _dot(h, wg_ref[...])
    u = _dot(h, wu_ref[...])
    a = (g * jax.nn.sigmoid(g) * u * 0.5).astype(BF16)
    o_ref[...] += _dot(a, wd_ref[...])

    _cast_blocks(cast_in, cast_out)

    if final_norm:
        @pl.when(f == pl.num_programs(1) - 1)
        def _():
            o_ref[...] = _rms(o_ref[...], gf_ref[...])


def _ffn(x, gain, wg, wu, wd, final_gain, *, final_norm, tm, tf, cast=()):
    m, d = x.shape
    dff = wg.shape[1]
    n_i, n_f = m // tm, dff // tf
    cast_in_specs, cast_out_specs, cast_shapes = _cast_streams(cast, n_i * n_f, lambda i, f: i * n_f + f)
    outs = pl.pallas_call(
        functools.partial(_ffn_kernel, final_norm=final_norm, n_cast=len(cast)),
        grid=(n_i, n_f),
        in_specs=[
            pl.BlockSpec((tm, d), lambda i, f: (i, 0)),
            pl.BlockSpec((1, d), lambda i, f: (0, 0)),
            pl.BlockSpec((d, tf), lambda i, f: (0, f)),
            pl.BlockSpec((d, tf), lambda i, f: (0, f)),
            pl.BlockSpec((tf, d), lambda i, f: (f, 0)),
            pl.BlockSpec((1, d), lambda i, f: (0, 0)),
        ] + cast_in_specs,
        out_specs=[pl.BlockSpec((tm, d), lambda i, f: (i, 0))] + cast_out_specs,
        out_shape=[jax.ShapeDtypeStruct((m, d), F32)] + cast_shapes,
        scratch_shapes=[pltpu.VMEM((tm, d), BF16)],
        compiler_params=_params(("arbitrary", "arbitrary")),
        name="ffn_final" if final_norm else "ffn",
    )(x, gain, wg, wu, wd, final_gain, *cast)
    return outs[0], outs[1:]


def _proj_kernel(x_ref, g_ref, w_ref, o_ref, h_ref, *, group_width):
    @pl.when(pl.program_id(1) == 0)
    def _():
        h_ref[...] = _rms(x_ref[...], g_ref[...]).astype(BF16)

    res = _dot(h_ref[...], w_ref[...]).astype(o_ref.dtype)
    for k in range(o_ref.shape[0]):
        o_ref[k] = res[:, k * group_width:(k + 1) * group_width]


def _proj(x, gain, w, *, group_width, tm, tn):
    m, d = x.shape
    n = w.shape[1]
    gpb = tn // group_width
    return pl.pallas_call(
        functools.partial(_proj_kernel, group_width=group_width),
        grid=(m // tm, n // tn),
        in_specs=[
            pl.BlockSpec((tm, d), lambda i, j: (i, 0)),
            pl.BlockSpec((1, d), lambda i, j: (0, 0)),
            pl.BlockSpec((d, tn), lambda i, j: (0, j)),
        ],
        out_specs=pl.BlockSpec((gpb, tm, group_width), lambda i, j: (j, i, 0)),
        out_shape=jax.ShapeDtypeStruct((n // group_width, m, group_width), BF16),
        scratch_shapes=[pltpu.VMEM((tm, d), BF16)],
        compiler_params=_params(("arbitrary", "arbitrary")),
        name="in_proj",
    )(x, gain, w)


def _conv_rows(b, c, v, prev, w, ubuf):
    t = b.shape[0]
    pad = F32_SUBLANES
    u = c * v
    ubuf[pad - (CONV_WIDTH - 1):pad, :] = prev
    ubuf[pad:pad + t, :] = u
    acc = w[CONV_WIDTH - 1:CONV_WIDTH] * u
    for j in range(CONV_WIDTH - 1):
        shift = CONV_WIDTH - 1 - j
        acc = acc + w[j:j + 1] * ubuf[pad - shift:pad - shift + t, :]
    return b * acc, u[t - (CONV_WIDTH - 1):, :]


def _hgrn_tables(group, width):
    nb = group // HGRN_BLOCK
    halves = []
    h = HGRN_BLOCK
    while h < group:
        halves.append(h)
        h *= 2
    assert halves, "group must span at least two blocks"
    tok = np.arange(group)

    def span(b0, b1):
        return ((tok >= b0 * HGRN_BLOCK) & (tok < b1 * HGRN_BLOCK)).astype(np.float32)

    zero = np.zeros(group, np.float32)
    rows, valid, index = [], [], {}
    for h in halves:
        hb = h // HGRN_BLOCK
        krows, kval, qrows, qval = [], [], [], []
        for b in range(nb):
            bmid = (b // (2 * hb)) * 2 * hb + hb
            first = b < bmid
            krows.append(span(b, bmid) if first else zero)
            kval.append(1.0 if first else 0.0)
            qrows.append(zero if first else span(bmid, b))
            qval.append(0.0 if first else 1.0)
        index[("k", h)] = len(rows)
        rows += krows
        valid += kval
        if hb > 1:
            index[("q", h)] = len(rows)
            rows += qrows
            valid += qval
    index["q_in"] = len(rows)
    rows += [span(0, b) for b in range(nb)]
    valid += [1.0] * nb
    index["k_out"] = len(rows)
    rows += [span(b, nb) for b in range(nb)]
    valid += [1.0] * nb
    index["decay"] = len(rows)
    rows.append(span(0, nb))
    valid.append(1.0)
    nx = -(-len(rows) // BF16_SUBLANES) * BF16_SUBLANES
    while len(rows) < nx:
        rows.append(zero)
        valid.append(0.0)

    local = ((tok[:, None] // HGRN_BLOCK == tok[None, :] // HGRN_BLOCK) & (tok[None, :] <= tok[:, None]))
    lhs = np.concatenate([local.astype(np.float32), np.stack(rows)], axis=0)
    lhs = np.concatenate([lhs, lhs], axis=1)
    valid = np.broadcast_to(np.asarray(valid, np.float32)[:, None], (nx, width)).copy()

    rb, cb = tok[:, None] // HGRN_BLOCK, tok[None, :] // HGRN_BLOCK
    cls = np.full((group, group), len(halves), np.int32)
    for li, h in enumerate(halves[:-1]):
        hb = h // HGRN_BLOCK
        own = (rb // (2 * hb) == cb // (2 * hb)) & ((rb // hb) % 2 == 1) & ((cb // hb) % 2 == 0)
        cls[own] = li + 1
    cls[local] = 0
    return lhs, valid, cls, index, halves


def _hgrn_kernel(*refs, layer, group, seqs, seq_rows, index, halves, n_cast):
    q_ref, f_ref, i_ref, og_ref, lbl_ref, nrm_ref, s0_ref, lhs_ref, valid_ref, cls_ref = refs[:10]
    cast_in = refs[10:10 + n_cast]
    y_ref, s_ref = refs[10 + n_cast:12 + n_cast]
    cast_out = refs[12 + n_cast:12 + 2 * n_cast]
    st_ref, fc_ref = refs[12 + 2 * n_cast:]
    chunk = pl.program_id(2)
    width = HEADS_PER_STEP * HEAD_DIM
    nb = group // HGRN_BLOCK
    _cast_blocks(cast_in, cast_out)

    @pl.when(chunk == 0)
    def _():
        for j in range(seqs):
            for hh in range(HEADS_PER_STEP):
                st_ref[j, hh] = s0_ref[j, hh].T

    logits = lbl_ref[...]
    e = jnp.exp(logits - jnp.max(logits, axis=0, keepdims=True))
    lb = jnp.sum(e[:layer + 1], axis=0, keepdims=True) / jnp.sum(e, axis=0, keepdims=True)
    lb1 = 1.0 - lb
    gain = nrm_ref[...]
    lhs = lhs_ref[...]
    valid = valid_ref[...]
    cls = cls_ref[...]
    masks = [cls == ci for ci in range(len(halves))]

    n_items = seqs * (seq_rows // group)

    def rows_of(item):
        j, g = divmod(item, seq_rows // group)
        return j, slice(j * seq_rows + g * group, j * seq_rows + (g + 1) * group)

    def front(item):
        j, rs = rows_of(item)
        q = q_ref[0, rs, :].astype(F32)
        fz = f_ref[0, rs, :].astype(F32)
        v = i_ref[0, rs, :]
        og = og_ref[0, rs, :].astype(F32)

        f = lb + lb1 * jax.nn.sigmoid(fz)
        lg = jnp.log(f) * INV_LN2
        k = 1.0 - f
        qf = q * jax.nn.sigmoid(q)

        top = pltpu.bitcast(pltpu.bitcast(lg, jnp.uint32) & jnp.uint32(BF16_BITS_OF_F32), F32)
        pieces = jnp.concatenate([top.astype(BF16), (lg - top).astype(BF16)], axis=0)
        sums = _dot(lhs, pieces)
        e_loc = sums[:group]
        fc_ref[item] = jnp.exp2(sums[group:]) * valid

        def expand(name):
            base = index[name]
            tiles = []
            for b in range(nb):
                tile = jnp.broadcast_to(fc_ref[item, pl.ds(base + b, 1), :], (F32_SUBLANES, width))
                tiles.append(jnp.concatenate([tile] * (HGRN_BLOCK // F32_SUBLANES), axis=0).astype(BF16))
            return jnp.concatenate(tiles, axis=0)

        qd_b = (qf * jnp.exp2(e_loc)).astype(BF16)
        kd_b = (k * jnp.exp2(-e_loc)).astype(BF16)
        q_lv = [qd_b if h == HGRN_BLOCK else qd_b * expand(("q", h)) for h in halves]
        k_lv = [kd_b * expand(("k", h)) for h in halves]
        q_in = qd_b * expand("q_in")
        k_out = kd_b * expand("k_out")
        sog = (og * jax.nn.sigmoid(og)) * gain
        return qd_b, kd_b, q_lv, k_lv, q_in, k_out, v, sog

    def mid(ops):
        qd_b, kd_b, q_lv, k_lv, q_in, k_out, v, sog = ops
        a_heads = []
        for hh in range(HEADS_PER_STEP):
            sl = slice(hh * HEAD_DIM, (hh + 1) * HEAD_DIM)
            if group % 128 == 0:
                both = _dot_nt(qd_b[:, sl], jnp.concatenate([kd_b[:, sl], k_lv[0][:, sl]], axis=0))
                parts = [both[:, :group], both[:, group:]]
            else:
                parts = [_dot_nt(qd_b[:, sl], kd_b[:, sl]), _dot_nt(qd_b[:, sl], k_lv[0][:, sl])]
            parts += [_dot_nt(ql[:, sl], kl[:, sl]) for ql, kl in zip(q_lv[1:], k_lv[1:])]
            a = parts[-1]
            for ci in reversed(range(len(halves))):
                a = jnp.where(masks[ci], parts[ci], a)
            a_heads.append(a.astype(BF16))
        return a_heads, q_in, k_out, v, sog

    def back(item, ops):
        a_heads, q_in, k_out, v, sog = ops
        j, rs = rows_of(item)
        decay = fc_ref[item, pl.ds(index["decay"], 1), :]
        outs = []
        for hh in range(HEADS_PER_STEP):
            sl = slice(hh * HEAD_DIM, (hh + 1) * HEAD_DIM)
            st = st_ref[j, hh]
            o = _dot(a_heads[hh], v[:, sl]) + _dot_nt(q_in[:, sl], st.astype(BF16))
            st_ref[j, hh] = st * decay[:, sl] + _dot_tn(v[:, sl], k_out[:, sl])
            outs.append(o * lax.rsqrt(jnp.mean(o * o, axis=-1, keepdims=True) + NORM_EPS))
        y_ref[0, rs, :] = (jnp.concatenate(outs, axis=1) * sog).astype(y_ref.dtype)

    fronts, mids = {}, {}
    for t in range(n_items + 2):
        if t < n_items:
            fronts[t] = front(t)
        if 0 <= t - 1 < n_items:
            mids[t - 1] = mid(fronts.pop(t - 1))
        if 0 <= t - 2 < n_items:
            back(t - 2, mids.pop(t - 2))

    @pl.when(chunk == pl.num_programs(2) - 1)
    def _():
        for j in range(seqs):
            for hh in range(HEADS_PER_STEP):
                s_ref[j, hh] = st_ref[j, hh].T


def _hgrn(p, lb_logits, hgrn_norm, s0, *, first_group, layer, batch, seqs, tc, group, cast=()):
    _, m, width = p.shape
    npairs = s0.shape[1] // HEADS_PER_STEP
    t = m // batch
    nc = t // tc
    assert seqs == 1 or nc == 1
    rows = seqs * tc
    lhs, valid, cls, index, halves = _hgrn_tables(group, width)
    row = lambda b, c: b * nc + c
    part = lambda k: pl.BlockSpec((1, rows, width), lambda b, hp, c: (first_group + k * npairs + hp, row(b, c), 0))
    state_spec = pl.BlockSpec((seqs, HEADS_PER_STEP, HEAD_DIM, HEAD_DIM), lambda b, hp, c: (b, hp, 0, 0))
    const = lambda a: pl.BlockSpec(a.shape, lambda b, hp, c: (0, 0))
    n_items = seqs * (tc // group)
    grid = (batch // seqs, npairs, nc)
    cast_in_specs, cast_out_specs, cast_shapes = _cast_streams(
        cast, grid[0] * grid[1] * grid[2], lambda b, hp, c: (b * npairs + hp) * nc + c)
    outs = pl.pallas_call(
        functools.partial(_hgrn_kernel, layer=layer, group=group, seqs=seqs, seq_rows=tc, index=index,
                          halves=halves, n_cast=len(cast)),
        grid=grid,
        in_specs=[
            part(0), part(1), part(2), part(3),
            pl.BlockSpec((lb_logits.shape[0], width), lambda b, hp, c: (0, hp)),
            pl.BlockSpec((1, width), lambda b, hp, c: (0, hp)),
            state_spec, const(lhs), const(valid), const(cls),
        ] + cast_in_specs,
        out_specs=[
            pl.BlockSpec((1, rows, width), lambda b, hp, c: (hp, row(b, c), 0)),
            state_spec,
        ] + cast_out_specs,
        out_shape=[
            jax.ShapeDtypeStruct((npairs, m, width), BF16),
            jax.ShapeDtypeStruct(s0.shape, F32),
        ] + cast_shapes,
        scratch_shapes=[
            pltpu.VMEM((seqs, HEADS_PER_STEP, HEAD_DIM, HEAD_DIM), F32),
            pltpu.VMEM((n_items, valid.shape[0], width), F32),
        ],
        compiler_params=_params(("arbitrary", "arbitrary", "arbitrary")),
        name="hgrn2",
    )(p, p, p, p, lb_logits, hgrn_norm, s0, jnp.asarray(lhs, BF16), jnp.asarray(valid), jnp.asarray(cls), *cast)
    return outs[0], outs[1], outs[2:]


def _merge_kernel(b_ref, c_ref, v_ref, c0_ref, cw_ref, yh_ref, gc0_ref, gc1_ref, gh0_ref, gh1_ref, x_ref,
                  wbc_ref, wbh_ref, wo_ref, o_ref, tail_ref, ubuf, *, tiles_per_seq):
    @pl.when(pl.program_id(0) % tiles_per_seq == 0)
    def _():
        tail_ref[...] = c0_ref[...]

    seqs = tail_ref.shape[0]
    t = x_ref.shape[0] // seqs
    ng = b_ref.shape[0]
    wide = lambda ref, rs: jnp.concatenate([ref[g, rs, :] for g in range(ref.shape[0])], axis=1).astype(F32)

    def branch(groups, w_ref):
        acc = _dot(groups[0], w_ref[0])
        for g in range(1, len(groups)):
            acc = acc + _dot(groups[g], w_ref[g])
        return acc

    def gate(*halves):
        return jax.nn.sigmoid(jnp.concatenate([r[g] for r in halves for g in range(r.shape[0])], axis=1).astype(F32))

    bh = branch([yh_ref[g] for g in range(yh_ref.shape[0])], wbh_ref)
    cw = cw_ref[...]
    ys = []
    for s in range(seqs):
        rs = slice(s * t, (s + 1) * t)
        y, tail = _conv_rows(wide(b_ref, rs), wide(c_ref, rs), wide(v_ref, rs), tail_ref[s], cw, ubuf)
        tail_ref[s] = tail
        ys.append(y.astype(BF16))
    yc = jnp.concatenate(ys, axis=0)
    bc = branch([yc[:, g * LANE_GROUP:(g + 1) * LANE_GROUP] for g in range(ng)], wbc_ref)
    merged = (gate(gc0_ref, gc1_ref) * bc + gate(gh0_ref, gh1_ref) * bh).astype(BF16)
    o_ref[...] = x_ref[...] + _dot(merged, wo_ref[...])


def _merge(p, yh, x, conv0, conv_w, wbc, wbh, wo, *, conv_group, gate_group, batch, tm):
    m, d = x.shape
    gw = p.shape[-1]
    dc = conv0.shape[-1]
    ng = dc // gw
    nhalf = d // gw // 2
    assert gate_group % nhalf == 0 and conv_group % ng == 0
    t = m // batch
    seqs_per_tile, tiles_per_seq = max(1, tm // t), max(1, t // tm)
    const = lambda shape: pl.BlockSpec(shape, lambda i: (0,) * len(shape), pipeline_mode=pl.Buffered(1))
    conv_part = lambda k: pl.BlockSpec((ng, tm, gw), lambda i: (conv_group // ng + k, i, 0))
    gate = lambda k: pl.BlockSpec((nhalf, tm, gw), lambda i: (gate_group // nhalf + k, i, 0))
    tail_spec = pl.BlockSpec((seqs_per_tile, CONV_WIDTH - 1, dc), lambda i: (i // tiles_per_seq, 0, 0))
    return pl.pallas_call(
        functools.partial(_merge_kernel, tiles_per_seq=tiles_per_seq),
        grid=(m // tm,),
        in_specs=[
            conv_part(0), conv_part(1), conv_part(2), tail_spec,
            pl.BlockSpec((CONV_WIDTH, dc), lambda i: (0, 0)),
            pl.BlockSpec((yh.shape[0], tm, gw), lambda i: (0, i, 0)),
            gate(0), gate(1), gate(2), gate(3),
            pl.BlockSpec((tm, d), lambda i: (i, 0)),
            const(wbc.shape), const(wbh.shape), const(wo.shape),
        ],
        out_specs=[pl.BlockSpec((tm, d), lambda i: (i, 0)), tail_spec],
        out_shape=[jax.ShapeDtypeStruct((m, d), F32), jax.ShapeDtypeStruct(conv0.shape, F32)],
        scratch_shapes=[pltpu.VMEM((tm // seqs_per_tile + F32_SUBLANES, dc), F32)],
        compiler_params=_params(("arbitrary",)),
        name="conv_merge_out",
    )(p, p, p, conv0, conv_w, yh, p, p, p, p, x, wbc, wbh, wo)


MAX_ROW_TILE = 1024
FFN_CHUNK = 512
PROJ_CHUNK = 2560
MERGE_ROW_TILE = 256
HGRN_GROUP = 128
HGRN_CHUNK = 4096


def _plan_tiles(batch, t):
    m = batch * t
    tiles = dict(tm_ffn=min(MAX_ROW_TILE, m), tf=FFN_CHUNK, tm_proj=min(MAX_ROW_TILE, m), tn_proj=PROJ_CHUNK,
                 tm_merge=min(MERGE_ROW_TILE, m))
    if t >= HGRN_GROUP:
        tiles.update(seqs_hgrn=1, tc_hgrn=min(HGRN_CHUNK, t), group=HGRN_GROUP)
    else:
        tiles.update(seqs_hgrn=batch, tc_hgrn=t, group=t)
    return tiles


def _mix_and_ffn2(x1, conv0, s0, w, later, *, batch, tm_ffn, tf, tm_proj, tn_proj, tm_merge, seqs_hgrn, tc_hgrn, group):
    d = x1.shape[1]
    groups = w["in_groups"]
    p = _proj(x1, w["norm_mix"], w["in"], group_width=LANE_GROUP, tm=tm_proj, tn=tn_proj)
    pending = [a for a in later if a.dtype != BF16]
    yh, s_new, done = _hgrn(p, w["lb_logits"], w["hgrn_norm"], s0, first_group=groups["hgrn"], layer=0, batch=batch,
                            seqs=seqs_hgrn, tc=tc_hgrn, group=group, cast=pending)
    later = tuple(done) if pending else later
    wbc, wbh, wo, w2g, w2u, w2d = later
    grouped = lambda a: a.reshape(a.shape[0] // LANE_GROUP, LANE_GROUP, d)
    x2, conv_new = _merge(p, yh, x1, conv0, w["conv_w"], grouped(wbc), grouped(wbh), wo, conv_group=groups["conv"],
                          gate_group=groups["gate"], batch=batch, tm=tm_merge)
    y, _ = _ffn(x2, w["norm_ffn2"], w2g, w2u, w2d, w["norm_final"], final_norm=True, tm=tm_ffn, tf=tf)
    return y, conv_new, s_new, later


def _forward(x_prompt, x_sample, cache_conv, state_hgrn, norm_ffn1, w_ffn1_gate, w_ffn1_up, w_ffn1_down, norm_mix, w_in,
             conv_w, hgrn_lb_logits, hgrn_norm, w_br_conv, w_br_hgrn, w_out, norm_ffn2, w_ffn2_gate, w_ffn2_up,
             w_ffn2_down, norm_final, *, prompt_tiles=None, sample_tiles=None):
    assert w_in.shape[0] == 1, "single-layer trunk"
    prompt_tiles = prompt_tiles or _plan_tiles(*x_prompt.shape[:2])
    sample_tiles = sample_tiles or _plan_tiles(*x_sample.shape[:2])
    heads, dk, dv = state_hgrn.shape[-3:]
    assert dk == HEAD_DIM and dv == HEAD_DIM and heads % HEADS_PER_STEP == 0
    d = x_prompt.shape[-1]
    dc, dh = cache_conv.shape[-1], heads * dk
    bf = lambda a: a.astype(BF16)
    gf = norm_final.reshape(1, d)
    ffn1 = (norm_ffn1, bf(w_ffn1_gate[0]), bf(w_ffn1_up[0]), bf(w_ffn1_down[0]), gf)

    bp, tp, _ = x_prompt.shape
    x1p, (win,) = _ffn(x_prompt.reshape(bp * tp, d), *ffn1, final_norm=False, tm=prompt_tiles["tm_ffn"],
                       tf=prompt_tiles["tf"], cast=(w_in,))
    n_conv = 3 * dc
    w = dict(
        norm_mix=norm_mix, **{"in": win},
        in_groups=dict(conv=0, hgrn=n_conv // LANE_GROUP, gate=(n_conv + 4 * dh) // LANE_GROUP),
        conv_w=conv_w[0], lb_logits=hgrn_lb_logits, hgrn_norm=hgrn_norm, norm_ffn2=norm_ffn2, norm_final=gf,
    )
    later = (w_br_conv, w_br_hgrn, w_out, w_ffn2_gate, w_ffn2_up, w_ffn2_down)
    zero_conv = jnp.zeros((bp,) + cache_conv.shape[2:], F32)
    zero_hgrn = jnp.zeros((bp,) + state_hgrn.shape[2:], F32)
    yp, cp, sp, later = _mix_and_ffn2(x1p, zero_conv, zero_hgrn, w, later, batch=bp, **prompt_tiles)

    bs, ts, _ = x_sample.shape
    x1s, _ = _ffn(x_sample.reshape(bs * ts, d), *ffn1, final_norm=False, tm=sample_tiles["tm_ffn"], tf=sample_tiles["tf"])
    ys, cs, ss, _ = _mix_and_ffn2(x1s, cache_conv[0], state_hgrn[0], w, later, batch=bs, **sample_tiles)
    return yp.reshape(bp, tp, d), ys.reshape(bs, ts, d), cp[None], sp[None], cs[None], ss[None]


def kernel(x_prompt, x_sample, cache_conv, state_hgrn, norm_ffn1, w_ffn1_gate, w_ffn1_up, w_ffn1_down, norm_mix, w_in, conv_w, hgrn_lb_logits, hgrn_norm, w_br_conv, w_br_hgrn, w_out, norm_ffn2, w_ffn2_gate, w_ffn2_up, w_ffn2_down, norm_final):
    return _forward(
        x_prompt, x_sample, cache_conv, state_hgrn, norm_ffn1, w_ffn1_gate, w_ffn1_up, w_ffn1_down, norm_mix, w_in,
        conv_w, hgrn_lb_logits, hgrn_norm, w_br_conv, w_br_hgrn, w_out, norm_ffn2, w_ffn2_gate, w_ffn2_up, w_ffn2_down,
        norm_final)
```

```python
import functools

import numpy as np
import jax
import jax.numpy as jnp
from jax import lax
from jax.experimental import pallas as pl
from jax.experimental.pallas import tpu as pltpu

NORM_EPS = 1e-6
INV_LN2 = 1.4426950408889634
CONV_WIDTH = 3
HGRN_BLOCK = 16
HEAD_DIM = 128
HEADS_PER_STEP = 2
LANE_GROUP = HEADS_PER_STEP * HEAD_DIM
LANES = 128
F32_SUBLANES = 8
BF16_SUBLANES = 16
BF16_BITS_OF_F32 = 0xFFFF0000
VMEM_LIMIT_BYTES = 56 * 1024 * 1024

F32 = jnp.float32
BF16 = jnp.bfloat16


def _params(semantics):
    return pltpu.CompilerParams(dimension_semantics=semantics, vmem_limit_bytes=VMEM_LIMIT_BYTES)


def _rms(x, gain):
    r = lax.rsqrt(jnp.mean(x * x, axis=-1, keepdims=True) + NORM_EPS)
    return x * r * gain


def _dot(a, b):
    return jnp.dot(a, b, preferred_element_type=F32)


def _dot_nt(a, b):
    return lax.dot_general(a, b, (((1,), (1,)), ((), ())), preferred_element_type=F32)


def _dot_tn(a, b):
    return lax.dot_general(a, b, (((0,), (0,)), ((), ())), preferred_element_type=F32)


def _cast_streams(arrays, n_steps, step_of):
    in_specs, out_specs, shapes = [], [], []
    for a in arrays:
        _, rows, cols = a.shape
        cuts = [(nr, nc) for nr in range(1, rows // BF16_SUBLANES + 1) if rows % (nr * BF16_SUBLANES) == 0
                for nc in range(1, cols // LANES + 1) if cols % (nc * LANES) == 0
                if nr * nc <= n_steps]
        nr, nc = max(cuts, key=lambda c: (c[0] * c[1], c[0]))

        def block(*ids, nr=nr, nc=nc):
            s = jnp.minimum(step_of(*ids), nr * nc - 1)
            return s // nc, s % nc

        in_specs.append(pl.BlockSpec((None, rows // nr, cols // nc), lambda *ids, block=block: (0,) + block(*ids)))
        out_specs.append(pl.BlockSpec((rows // nr, cols // nc), block))
        shapes.append(jax.ShapeDtypeStruct((rows, cols), BF16))
    return in_specs, out_specs, shapes


def _cast_blocks(in_refs, out_refs):
    for src_ref, dst_ref in zip(in_refs, out_refs):
        dst_ref[...] = src_ref[...].astype(BF16)


def _ffn_kernel(*refs, final_norm, n_cast):
    x_ref, g_ref, wg_ref, wu_ref, wd_ref, gf_ref = refs[:6]
    cast_in = refs[6:6 + n_cast]
    o_ref = refs[6 + n_cast]
    cast_out = refs[7 + n_cast:7 + 2 * n_cast]
    h_ref = refs[7 + 2 * n_cast]
    f = pl.program_id(1)

    @pl.when(f == 0)
    def _():
        x = x_ref[...]
        h_ref[...] = _rms(x, g_ref[...]).astype(BF16)
        o_ref[...] = x

    h = h_ref[...]
    g = _dot(h, wg_ref[...])
    u = _dot(h, wu_ref[...])
    a = (g * jax.nn.sigmoid(g) * u * 0.5).astype(BF16)
    o_ref[...] += _dot(a, wd_ref[...])

    _cast_blocks(cast_in, cast_out)

    if final_norm:
        @pl.when(f == pl.num_programs(1) - 1)
        def _():
            o_ref[...] = _rms(o_ref[...], gf_ref[...])


def _ffn(x, gain, wg, wu, wd, final_gain, *, final_norm, tm, tf, cast=()):
    m, d = x.shape
    dff = wg.shape[1]
    n_i, n_f = m // tm, dff // tf
    cast_in_specs, cast_out_specs, cast_shapes = _cast_streams(cast, n_i * n_f, lambda i, f: i * n_f + f)
    outs = pl.pallas_call(
        functools.partial(_ffn_kernel, final_norm=final_norm, n_cast=len(cast)),
        grid=(n_i, n_f),
        in_specs=[
            pl.BlockSpec((tm, d), lambda i, f: (i, 0)),
            pl.BlockSpec((1, d), lambda i, f: (0, 0)),
            pl.BlockSpec((d, tf), lambda i, f: (0, f)),
            pl.BlockSpec((d, tf), lambda i, f: (0, f)),
            pl.BlockSpec((tf, d), lambda i, f: (f, 0)),
            pl.BlockSpec((1, d), lambda i, f: (0, 0)),
        ] + cast_in_specs,
        out_specs=[pl.BlockSpec((tm, d), lambda i, f: (i, 0))] + cast_out_specs,
        out_shape=[jax.ShapeDtypeStruct((m, d), F32)] + cast_shapes,
        scratch_shapes=[pltpu.VMEM((tm, d), BF16)],
        compiler_params=_params(("arbitrary", "arbitrary")),
        name="ffn_final" if final_norm else "ffn",
    )(x, gain, wg, wu, wd, final_gain, *cast)
    return outs[0], outs[1:]


def _proj_kernel(x_ref, g_ref, w_ref, o_ref, h_ref, *, group_width):
    @pl.when(pl.program_id(1) == 0)
    def _():
        h_ref[...] = _rms(x_ref[...], g_ref[...]).astype(BF16)

    res = _dot(h_ref[...], w_ref[...]).astype(o_ref.dtype)
    for k in range(o_ref.shape[0]):
        o_ref[k] = res[:, k * group_width:(k + 1) * group_width]


def _proj(x, gain, w, *, group_width, tm, tn):
    m, d = x.shape
    n = w.shape[1]
    gpb = tn // group_width
    return pl.pallas_call(
        functools.partial(_proj_kernel, group_width=group_width),
        grid=(m // tm, n // tn),
        in_specs=[
            pl.BlockSpec((tm, d), lambda i, j: (i, 0)),
            pl.BlockSpec((1, d), lambda i, j: (0, 0)),
            pl.BlockSpec((d, tn), lambda i, j: (0, j)),
        ],
        out_specs=pl.BlockSpec((gpb, tm, group_width), lambda i, j: (j, i, 0)),
        out_shape=jax.ShapeDtypeStruct((n // group_width, m, group_width), BF16),
        scratch_shapes=[pltpu.VMEM((tm, d), BF16)],
        compiler_params=_params(("arbitrary", "arbitrary")),
        name="in_proj",
    )(x, gain, w)


def _conv_rows(b, c, v, prev, w, ubuf):
    t = b.shape[0]
    pad = F32_SUBLANES
    u = c * v
    ubuf[pad - (CONV_WIDTH - 1):pad, :] = prev
    ubuf[pad:pad + t, :] = u
    acc = w[CONV_WIDTH - 1:CONV_WIDTH] * u
    for j in range(CONV_WIDTH - 1):
        shift = CONV_WIDTH - 1 - j
        acc = acc + w[j:j + 1] * ubuf[pad - shift:pad - shift + t, :]
    return b * acc, u[t - (CONV_WIDTH - 1):, :]


def _hgrn_tables(group, width):
    nb = group // HGRN_BLOCK
    halves = []
    h = HGRN_BLOCK
    while h < group:
        halves.append(h)
        h *= 2
    assert halves, "group must span at least two blocks"
    tok = np.arange(group)

    def span(b0, b1):
        return ((tok >= b0 * HGRN_BLOCK) & (tok < b1 * HGRN_BLOCK)).astype(np.float32)

    zero = np.zeros(group, np.float32)
    rows, valid, index = [], [], {}
    for h in halves:
        hb = h // HGRN_BLOCK
        krows, kval, qrows, qval = [], [], [], []
        for b in range(nb):
            bmid = (b // (2 * hb)) * 2 * hb + hb
            first = b < bmid
            krows.append(span(b, bmid) if first else zero)
            kval.append(1.0 if first else 0.0)
            qrows.append(zero if first else span(bmid, b))
            qval.append(0.0 if first else 1.0)
        index[("k", h)] = len(rows)
        rows += krows
        valid += kval
        if hb > 1:
            index[("q", h)] = len(rows)
            rows += qrows
            valid += qval
    index["q_in"] = len(rows)
    rows += [span(0, b) for b in range(nb)]
    valid += [1.0] * nb
    index["k_out"] = len(rows)
    rows += [span(b, nb) for b in range(nb)]
    valid += [1.0] * nb
    index["decay"] = len(rows)
    rows.append(span(0, nb))
    valid.append(1.0)
    nx = -(-len(rows) // BF16_SUBLANES) * BF16_SUBLANES
    while len(rows) < nx:
        rows.append(zero)
        valid.append(0.0)

    local = ((tok[:, None] // HGRN_BLOCK == tok[None, :] // HGRN_BLOCK) & (tok[None, :] <= tok[:, None]))
    lhs = np.concatenate([local.astype(np.float32), np.stack(rows)], axis=0)
    lhs = np.concatenate([lhs, lhs], axis=1)
    valid = np.broadcast_to(np.asarray(valid, np.float32)[:, None], (nx, width)).copy()

    rb, cb = tok[:, None] // HGRN_BLOCK, tok[None, :] // HGRN_BLOCK
    cls = np.full((group, group), len(halves), np.int32)
    for li, h in enumerate(halves[:-1]):
        hb = h // HGRN_BLOCK
        own = (rb // (2 * hb) == cb // (2 * hb)) & ((rb // hb) % 2 == 1) & ((cb // hb) % 2 == 0)
        cls[own] = li + 1
    cls[local] = 0
    return lhs, valid, cls, index, halves


def _hgrn_kernel(*refs, layer, group, seqs, seq_rows, index, halves, n_cast):
    q_ref, f_ref, i_ref, og_ref, lbl_ref, nrm_ref, s0_ref, lhs_ref, valid_ref, cls_ref = refs[:10]
    cast_in = refs[10:10 + n_cast]
    y_ref, s_ref = refs[10 + n_cast:12 + n_cast]
    cast_out = refs[12 + n_cast:12 + 2 * n_cast]
    st_ref, fc_ref = refs[12 + 2 * n_cast:]
    chunk = pl.program_id(2)
    width = HEADS_PER_STEP * HEAD_DIM
    nb = group // HGRN_BLOCK
    _cast_blocks(cast_in, cast_out)

    @pl.when(chunk == 0)
    def _():
        for j in range(seqs):
            for hh in range(HEADS_PER_STEP):
                st_ref[j, hh] = s0_ref[j, hh].T

    logits = lbl_ref[...]
    e = jnp.exp(logits - jnp.max(logits, axis=0, keepdims=True))
    lb = jnp.sum(e[:layer + 1], axis=0, keepdims=True) / jnp.sum(e, axis=0, keepdims=True)
    lb1 = 1.0 - lb
    gain = nrm_ref[...]
    lhs = lhs_ref[...]
    valid = valid_ref[...]
    cls = cls_ref[...]
    masks = [cls == ci for ci in range(len(halves))]

    n_items = seqs * (seq_rows // group)

    def rows_of(item):
        j, g = divmod(item, seq_rows // group)
        return j, slice(j * seq_rows + g * group, j * seq_rows + (g + 1) * group)

    def front(item):
        j, rs = rows_of(item)
        q = q_ref[0, rs, :].astype(F32)
        fz = f_ref[0, rs, :].astype(F32)
        v = i_ref[0, rs, :]
        og = og_ref[0, rs, :].astype(F32)

        f = lb + lb1 * jax.nn.sigmoid(fz)
        lg = jnp.log(f) * INV_LN2
        k = 1.0 - f
        qf = q * jax.nn.sigmoid(q)

        top = pltpu.bitcast(pltpu.bitcast(lg, jnp.uint32) & jnp.uint32(BF16_BITS_OF_F32), F32)
        pieces = jnp.concatenate([top.astype(BF16), (lg - top).astype(BF16)], axis=0)
        sums = _dot(lhs, pieces)
        e_loc = sums[:group]
        fc_ref[item] = jnp.exp2(sums[group:]) * valid

        def expand(name):
            base = index[name]
            tiles = []
            for b in range(nb):
                tile = jnp.broadcast_to(fc_ref[item, pl.ds(base + b, 1), :], (F32_SUBLANES, width))
                tiles.append(jnp.concatenate([tile] * (HGRN_BLOCK // F32_SUBLANES), axis=0).astype(BF16))
            return jnp.concatenate(tiles, axis=0)

        qd_b = (qf * jnp.exp2(e_loc)).astype(BF16)
        kd_b = (k * jnp.exp2(-e_loc)).astype(BF16)
        q_lv = [qd_b if h == HGRN_BLOCK else qd_b * expand(("q", h)) for h in halves]
        k_lv = [kd_b * expand(("k", h)) for h in halves]
        q_in = qd_b * expand("q_in")
        k_out = kd_b * expand("k_out")
        sog = (og * jax.nn.sigmoid(og)) * gain
        return qd_b, kd_b, q_lv, k_lv, q_in, k_out, v, sog

    def mid(ops):
        qd_b, kd_b, q_lv, k_lv, q_in, k_out, v, sog = ops
        a_heads = []
        for hh in range(HEADS_PER_STEP):
            sl = slice(hh * HEAD_DIM, (hh + 1) * HEAD_DIM)
            if group % 128 == 0:
                both = _dot_nt(qd_b[:, sl], jnp.concatenate([kd_b[:, sl], k_lv[0][:, sl]], axis=0))
                parts = [both[:, :group], both[:, group:]]
            else:
                parts = [_dot_nt(qd_b[:, sl], kd_b[:, sl]), _dot_nt(qd_b[:, sl], k_lv[0][:, sl])]
            parts += [_dot_nt(ql[:, sl], kl[:, sl]) for ql, kl in zip(q_lv[1:], k_lv[1:])]
            a = parts[-1]
            for ci in reversed(range(len(halves))):
                a = jnp.where(masks[ci], parts[ci], a)
            a_heads.append(a.astype(BF16))
        return a_heads, q_in, k_out, v, sog

    def back(item, ops):
        a_heads, q_in, k_out, v, sog = ops
        j, rs = rows_of(item)
        decay = fc_ref[item, pl.ds(index["decay"], 1), :]
        outs = []
        for hh in range(HEADS_PER_STEP):
            sl = slice(hh * HEAD_DIM, (hh + 1) * HEAD_DIM)
            st = st_ref[j, hh]
            o = _dot(a_heads[hh], v[:, sl]) + _dot_nt(q_in[:, sl], st.astype(BF16))
            st_ref[j, hh] = st * decay[:, sl] + _dot_tn(v[:, sl], k_out[:, sl])
            outs.append(o * lax.rsqrt(jnp.mean(o * o, axis=-1, keepdims=True) + NORM_EPS))
        y_ref[0, rs, :] = (jnp.concatenate(outs, axis=1) * sog).astype(y_ref.dtype)

    fronts, mids = {}, {}
    for t in range(n_items + 2):
        if t < n_items:
            fronts[t] = front(t)
        if 0 <= t - 1 < n_items:
            mids[t - 1] = mid(fronts.pop(t - 1))
        if 0 <= t - 2 < n_items:
            back(t - 2, mids.pop(t - 2))

    @pl.when(chunk == pl.num_programs(2) - 1)
    def _():
        for j in range(seqs):
            for hh in range(HEADS_PER_STEP):
                s_ref[j, hh] = st_ref[j, hh].T


def _hgrn(p, lb_logits, hgrn_norm, s0, *, first_group, layer, batch, seqs, tc, group, cast=()):
    _, m, width = p.shape
    npairs = s0.shape[1] // HEADS_PER_STEP
    t = m // batch
    nc = t // tc
    assert seqs == 1 or nc == 1
    rows = seqs * tc
    lhs, valid, cls, index, halves = _hgrn_tables(group, width)
    row = lambda b, c: b * nc + c
    part = lambda k: pl.BlockSpec((1, rows, width), lambda b, hp, c: (first_group + k * npairs + hp, row(b, c), 0))
    state_spec = pl.BlockSpec((seqs, HEADS_PER_STEP, HEAD_DIM, HEAD_DIM), lambda b, hp, c: (b, hp, 0, 0))
    const = lambda a: pl.BlockSpec(a.shape, lambda b, hp, c: (0, 0))
    n_items = seqs * (tc // group)
    grid = (batch // seqs, npairs, nc)
    cast_in_specs, cast_out_specs, cast_shapes = _cast_streams(
        cast, grid[0] * grid[1] * grid[2], lambda b, hp, c: (b * npairs + hp) * nc + c)
    outs = pl.pallas_call(
        functools.partial(_hgrn_kernel, layer=layer, group=group, seqs=seqs, seq_rows=tc, index=index,
                          halves=halves, n_cast=len(cast)),
        grid=grid,
        in_specs=[
            part(0), part(1), part(2), part(3),
            pl.BlockSpec((lb_logits.shape[0], width), lambda b, hp, c: (0, hp)),
            pl.BlockSpec((1, width), lambda b, hp, c: (0, hp)),
            state_spec, const(lhs), const(valid), const(cls),
        ] + cast_in_specs,
        out_specs=[
            pl.BlockSpec((1, rows, width), lambda b, hp, c: (hp, row(b, c), 0)),
            state_spec,
        ] + cast_out_specs,
        out_shape=[
            jax.ShapeDtypeStruct((npairs, m, width), BF16),
            jax.ShapeDtypeStruct(s0.shape, F32),
        ] + cast_shapes,
        scratch_shapes=[
            pltpu.VMEM((seqs, HEADS_PER_STEP, HEAD_DIM, HEAD_DIM), F32),
            pltpu.VMEM((n_items, valid.shape[0], width), F32),
        ],
        compiler_params=_params(("arbitrary", "arbitrary", "arbitrary")),
        name="hgrn2",
    )(p, p, p, p, lb_logits, hgrn_norm, s0, jnp.asarray(lhs, BF16), jnp.asarray(valid), jnp.asarray(cls), *cast)
    return outs[0], outs[1], outs[2:]


def _aligned_blocks(first, count):
    if first % count == 0:
        return [(first // count, count)]
    for head in range(1, count):
        if first % head == 0 and (first + head) % (count - head) == 0:
            return [(first // head, head), ((first + head) // (count - head), count - head)]
    raise ValueError(f"groups [{first}, {first + count}) cannot be covered by two aligned blocks")


def _merge_kernel(*refs, tiles_per_seq, n_gate_blocks):
    bcv_ref, c0_ref, cw_ref, yh_ref = refs[:4]
    gate_refs = refs[4:4 + n_gate_blocks]
    x_ref, wbc_ref, wbh_ref, wo_ref, o_ref, tail_ref, ubuf = refs[4 + n_gate_blocks:]

    @pl.when(pl.program_id(0) % tiles_per_seq == 0)
    def _():
        tail_ref[...] = c0_ref[...]

    seqs = tail_ref.shape[0]
    tm, d = x_ref.shape
    t = tm // seqs
    ng = bcv_ref.shape[0] // 3

    def wide(first, rs):
        return jnp.concatenate([bcv_ref[first + g, rs, :] for g in range(ng)], axis=1).astype(F32)

    def branch(groups, w_ref):
        acc = _dot(groups[0], w_ref[0])
        for g in range(1, len(groups)):
            acc = acc + _dot(groups[g], w_ref[g])
        return acc

    bh = branch([yh_ref[g] for g in range(yh_ref.shape[0])], wbh_ref)
    cw = cw_ref[...]
    ys = []
    for s in range(seqs):
        rs = slice(s * t, (s + 1) * t)
        y, tail = _conv_rows(wide(0, rs), wide(ng, rs), wide(2 * ng, rs), tail_ref[s], cw, ubuf)
        tail_ref[s] = tail
        ys.append(y.astype(BF16))
    yc = jnp.concatenate(ys, axis=0)
    bc = branch([yc[:, g * LANE_GROUP:(g + 1) * LANE_GROUP] for g in range(ng)], wbc_ref)
    gates = jax.nn.sigmoid(
        jnp.concatenate([r[g] for r in gate_refs for g in range(r.shape[0])], axis=1).astype(F32))
    merged = (gates[:, :d] * bc + gates[:, d:] * bh).astype(BF16)
    o_ref[...] = x_ref[...] + _dot(merged, wo_ref[...])


def _merge(p, yh, x, conv0, conv_w, wbc, wbh, wo, *, conv_group, gate_group, batch, tm):
    m, d = x.shape
    gw = p.shape[-1]
    dc = conv0.shape[-1]
    (conv_block, n_conv), = _aligned_blocks(conv_group, 3 * dc // gw)
    gate_blocks = _aligned_blocks(gate_group, 2 * d // gw)
    t = m // batch
    seqs_per_tile, tiles_per_seq = max(1, tm // t), max(1, t // tm)
    const = lambda shape: pl.BlockSpec(shape, lambda i: (0,) * len(shape), pipeline_mode=pl.Buffered(1))
    groups = lambda block, size: pl.BlockSpec((size, tm, gw), lambda i: (block, i, 0))
    tail_spec = pl.BlockSpec((seqs_per_tile, CONV_WIDTH - 1, dc), lambda i: (i // tiles_per_seq, 0, 0))
    return pl.pallas_call(
        functools.partial(_merge_kernel, tiles_per_seq=tiles_per_seq, n_gate_blocks=len(gate_blocks)),
        grid=(m // tm,),
        in_specs=[
            groups(conv_block, n_conv), tail_spec,
            pl.BlockSpec((CONV_WIDTH, dc), lambda i: (0, 0)),
            pl.BlockSpec((yh.shape[0], tm, gw), lambda i: (0, i, 0)),
            *[groups(block, size) for block, size in gate_blocks],
            pl.BlockSpec((tm, d), lambda i: (i, 0)),
            const(wbc.shape), const(wbh.shape), const(wo.shape),
        ],
        out_specs=[pl.BlockSpec((tm, d), lambda i: (i, 0)), tail_spec],
        out_shape=[jax.ShapeDtypeStruct((m, d), F32), jax.ShapeDtypeStruct(conv0.shape, F32)],
        scratch_shapes=[pltpu.VMEM((tm // seqs_per_tile + F32_SUBLANES, dc), F32)],
        compiler_params=_params(("arbitrary",)),
        name="conv_merge_out",
    )(p, conv0, conv_w, yh, *[p] * len(gate_blocks), x, wbc, wbh, wo)


MAX_ROW_TILE = 1024
FFN_CHUNK = 512
PROJ_CHUNK = 2560
MERGE_ROW_TILE = 256
HGRN_GROUP = 128
HGRN_CHUNK = 4096


def _plan_tiles(batch, t):
    m = batch * t
    tiles = dict(tm_ffn=min(MAX_ROW_TILE, m), tf=FFN_CHUNK, tm_proj=min(MAX_ROW_TILE, m), tn_proj=PROJ_CHUNK,
                 tm_merge=min(MERGE_ROW_TILE, m))
    if t >= HGRN_GROUP:
        tiles.update(seqs_hgrn=1, tc_hgrn=min(HGRN_CHUNK, t), group=HGRN_GROUP)
    else:
        tiles.update(seqs_hgrn=batch, tc_hgrn=t, group=t)
    return tiles


def _mix_and_ffn2(x1, conv0, s0, w, later, *, batch, tm_ffn, tf, tm_proj, tn_proj, tm_merge, seqs_hgrn, tc_hgrn, group):
    d = x1.shape[1]
    groups = w["in_groups"]
    p = _proj(x1, w["norm_mix"], w["in"], group_width=LANE_GROUP, tm=tm_proj, tn=tn_proj)
    pending = [a for a in later if a.dtype != BF16]
    yh, s_new, done = _hgrn(p, w["lb_logits"], w["hgrn_norm"], s0, first_group=groups["hgrn"], layer=0, batch=batch,
                            seqs=seqs_hgrn, tc=tc_hgrn, group=group, cast=pending)
    later = tuple(done) if pending else later
    wbc, wbh, wo, w2g, w2u, w2d = later
    grouped = lambda a: a.reshape(a.shape[0] // LANE_GROUP, LANE_GROUP, d)
    x2, conv_new = _merge(p, yh, x1, conv0, w["conv_w"], grouped(wbc), grouped(wbh), wo, conv_group=groups["conv"],
                          gate_group=groups["gate"], batch=batch, tm=tm_merge)
    y, _ = _ffn(x2, w["norm_ffn2"], w2g, w2u, w2d, w["norm_final"], final_norm=True, tm=tm_ffn, tf=tf)
    return y, conv_new, s_new, later


def _forward(x_prompt, x_sample, cache_conv, state_hgrn, norm_ffn1, w_ffn1_gate, w_ffn1_up, w_ffn1_down, norm_mix, w_in,
             conv_w, hgrn_lb_logits, hgrn_norm, w_br_conv, w_br_hgrn, w_out, norm_ffn2, w_ffn2_gate, w_ffn2_up,
             w_ffn2_down, norm_final, *, prompt_tiles=None, sample_tiles=None):
    assert w_in.shape[0] == 1, "single-layer trunk"
    prompt_tiles = prompt_tiles or _plan_tiles(*x_prompt.shape[:2])
    sample_tiles = sample_tiles or _plan_tiles(*x_sample.shape[:2])
    heads, dk, dv = state_hgrn.shape[-3:]
    assert dk == HEAD_DIM and dv == HEAD_DIM and heads % HEADS_PER_STEP == 0
    d = x_prompt.shape[-1]
    dc, dh = cache_conv.shape[-1], heads * dk
    bf = lambda a: a.astype(BF16)
    gf = norm_final.reshape(1, d)
    ffn1 = (norm_ffn1, bf(w_ffn1_gate[0]), bf(w_ffn1_up[0]), bf(w_ffn1_down[0]), gf)

    bp, tp, _ = x_prompt.shape
    x1p, (win,) = _ffn(x_prompt.reshape(bp * tp, d), *ffn1, final_norm=False, tm=prompt_tiles["tm_ffn"],
                       tf=prompt_tiles["tf"], cast=(w_in,))
    n_conv = 3 * dc
    w = dict(
        norm_mix=norm_mix, **{"in": win},
        in_groups=dict(conv=0, hgrn=n_conv // LANE_GROUP, gate=(n_conv + 4 * dh) // LANE_GROUP),
        conv_w=conv_w[0], lb_logits=hgrn_lb_logits, hgrn_norm=hgrn_norm, norm_ffn2=norm_ffn2, norm_final=gf,
    )
    later = (w_br_conv, w_br_hgrn, w_out, w_ffn2_gate, w_ffn2_up, w_ffn2_down)
    zero_conv = jnp.zeros((bp,) + cache_conv.shape[2:], F32)
    zero_hgrn = jnp.zeros((bp,) + state_hgrn.shape[2:], F32)
    yp, cp, sp, later = _mix_and_ffn2(x1p, zero_conv, zero_hgrn, w, later, batch=bp, **prompt_tiles)

    bs, ts, _ = x_sample.shape
    x1s, _ = _ffn(x_sample.reshape(bs * ts, d), *ffn1, final_norm=False, tm=sample_tiles["tm_ffn"], tf=sample_tiles["tf"])
    ys, cs, ss, _ = _mix_and_ffn2(x1s, cache_conv[0], state_hgrn[0], w, later, batch=bs, **sample_tiles)
    return yp.reshape(bp, tp, d), ys.reshape(bs, ts, d), cp[None], sp[None], cs[None], ss[None]


def kernel(x_prompt, x_sample, cache_conv, state_hgrn, norm_ffn1, w_ffn1_gate, w_ffn1_up, w_ffn1_down, norm_mix, w_in, conv_w, hgrn_lb_logits, hgrn_norm, w_br_conv, w_br_hgrn, w_out, norm_ffn2, w_ffn2_gate, w_ffn2_up, w_ffn2_down, norm_final):
    return _forward(
        x_prompt, x_sample, cache_conv, state_hgrn, norm_ffn1, w_ffn1_gate, w_ffn1_up, w_ffn1_down, norm_mix, w_in,
        conv_w, hgrn_lb_logits, hgrn_norm, w_br_conv, w_br_hgrn, w_out, norm_ffn2, w_ffn2_gate, w_ffn2_up, w_ffn2_down,
        norm_final)
```

```python
import functools

import numpy as np
import jax
import jax.numpy as jnp
from jax import lax
from jax.experimental import pallas as pl
from jax.experimental.pallas import tpu as pltpu

NORM_EPS = 1e-6
INV_LN2 = 1.4426950408889634
CONV_WIDTH = 3
HGRN_BLOCK = 16
HEAD_DIM = 128
HEADS_PER_STEP = 2
LANE_GROUP = HEADS_PER_STEP * HEAD_DIM
LANES = 128
F32_SUBLANES = 8
BF16_SUBLANES = 16
BF16_BITS_OF_F32 = 0xFFFF0000
VMEM_LIMIT_BYTES = 56 * 1024 * 1024

F32 = jnp.float32
BF16 = jnp.bfloat16


def _params(semantics):
    return pltpu.CompilerParams(dimension_semantics=semantics, vmem_limit_bytes=VMEM_LIMIT_BYTES)


def _rms(x, gain):
    r = lax.rsqrt(jnp.mean(x * x, axis=-1, keepdims=True) + NORM_EPS)
    return x * r * gain


def _dot(a, b):
    return jnp.dot(a, b, preferred_element_type=F32)


def _dot_nt(a, b):
    return lax.dot_general(a, b, (((1,), (1,)), ((), ())), preferred_element_type=F32)


def _dot_tn(a, b):
    return lax.dot_general(a, b, (((0,), (0,)), ((), ())), preferred_element_type=F32)


def _cast_streams(arrays, n_steps, step_of):
    in_specs, out_specs, shapes = [], [], []
    for a in arrays:
        _, rows, cols = a.shape
        cuts = [(nr, nc) for nr in range(1, rows // BF16_SUBLANES + 1) if rows % (nr * BF16_SUBLANES) == 0
                for nc in range(1, cols // LANES + 1) if cols % (nc * LANES) == 0
                if nr * nc <= n_steps]
        nr, nc = max(cuts, key=lambda c: (c[0] * c[1], c[0]))

        def block(*ids, nr=nr, nc=nc):
            s = jnp.minimum(step_of(*ids), nr * nc - 1)
            return s // nc, s % nc

        in_specs.append(pl.BlockSpec((None, rows // nr, cols // nc), lambda *ids, block=block: (0,) + block(*ids)))
        out_specs.append(pl.BlockSpec((rows // nr, cols // nc), block))
        shapes.append(jax.ShapeDtypeStruct((rows, cols), BF16))
    return in_specs, out_specs, shapes


def _cast_blocks(in_refs, out_refs):
    for src_ref, dst_ref in zip(in_refs, out_refs):
        dst_ref[...] = src_ref[...].astype(BF16)


def _ffn_kernel(*refs, final_norm, n_cast):
    x_ref, g_ref, wg_ref, wu_ref, wd_ref, gf_ref = refs[:6]
    cast_in = refs[6:6 + n_cast]
    o_ref = refs[6 + n_cast]
    cast_out = refs[7 + n_cast:7 + 2 * n_cast]
    h_ref = refs[7 + 2 * n_cast]
    f = pl.program_id(1)

    @pl.when(f == 0)
    def _():
        x = x_ref[...]
        h_ref[...] = _rms(x, g_ref[...]).astype(BF16)
        o_ref[...] = x

    h = h_ref[...]
    g = _dot(h, wg_ref[...])
    u = _dot(h, wu_ref[...])
    a = (g * jax.nn.sigmoid(g) * u * 0.5).astype(BF16)
    o_ref[...] += _dot(a, wd_ref[...])

    _cast_blocks(cast_in, cast_out)

    if final_norm:
        @pl.when(f == pl.num_programs(1) - 1)
        def _():
            o_ref[...] = _rms(o_ref[...], gf_ref[...])


def _ffn(x, gain, wg, wu, wd, final_gain, *, final_norm, tm, tf, cast=()):
    m, d = x.shape
    dff = wg.shape[1]
    n_i, n_f = m // tm, dff // tf
    cast_in_specs, cast_out_specs, cast_shapes = _cast_streams(cast, n_i * n_f, lambda i, f: i * n_f + f)
    outs = pl.pallas_call(
        functools.partial(_ffn_kernel, final_norm=final_norm, n_cast=len(cast)),
        grid=(n_i, n_f),
        in_specs=[
            pl.BlockSpec((tm, d), lambda i, f: (i, 0)),
            pl.BlockSpec((1, d), lambda i, f: (0, 0)),
            pl.BlockSpec((d, tf), lambda i, f: (0, f)),
            pl.BlockSpec((d, tf), lambda i, f: (0, f)),
            pl.BlockSpec((tf, d), lambda i, f: (f, 0)),
            pl.BlockSpec((1, d), lambda i, f: (0, 0)),
        ] + cast_in_specs,
        out_specs=[pl.BlockSpec((tm, d), lambda i, f: (i, 0))] + cast_out_specs,
        out_shape=[jax.ShapeDtypeStruct((m, d), F32)] + cast_shapes,
        scratch_shapes=[pltpu.VMEM((tm, d), BF16)],
        compiler_params=_params(("arbitrary", "arbitrary")),
        name="ffn_final" if final_norm else "ffn",
    )(x, gain, wg, wu, wd, final_gain, *cast)
    return outs[0], outs[1:]


def _proj_kernel(x_ref, g_ref, w_ref, o_ref, h_ref, *, group_width):
    @pl.when(pl.program_id(1) == 0)
    def _():
        h_ref[...] = _rms(x_ref[...], g_ref[...]).astype(BF16)

    res = _dot(h_ref[...], w_ref[...]).astype(o_ref.dtype)
    for k in range(o_ref.shape[0]):
        o_ref[k] = res[:, k * group_width:(k + 1) * group_width]


def _proj(x, gain, w, *, group_width, tm, tn):
    m, d = x.shape
    n = w.shape[1]
    gpb = tn // group_width
    return pl.pallas_call(
        functools.partial(_proj_kernel, group_width=group_width),
        grid=(m // tm, n // tn),
        in_specs=[
            pl.BlockSpec((tm, d), lambda i, j: (i, 0)),
            pl.BlockSpec((1, d), lambda i, j: (0, 0)),
            pl.BlockSpec((d, tn), lambda i, j: (0, j)),
        ],
        out_specs=pl.BlockSpec((gpb, tm, group_width), lambda i, j: (j, i, 0)),
        out_shape=jax.ShapeDtypeStruct((n // group_width, m, group_width), BF16),
        scratch_shapes=[pltpu.VMEM((tm, d), BF16)],
        compiler_params=_params(("arbitrary", "arbitrary")),
        name="in_proj",
    )(x, gain, w)


def _conv_rows(b, c, v, prev, w, ubuf):
    t = b.shape[0]
    pad = F32_SUBLANES
    u = c * v
    ubuf[pad - (CONV_WIDTH - 1):pad, :] = prev
    ubuf[pad:pad + t, :] = u
    acc = w[CONV_WIDTH - 1:CONV_WIDTH] * u
    for j in range(CONV_WIDTH - 1):
        shift = CONV_WIDTH - 1 - j
        acc = acc + w[j:j + 1] * ubuf[pad - shift:pad - shift + t, :]
    return b * acc, u[t - (CONV_WIDTH - 1):, :]


def _hgrn_tables(group, width):
    nb = group // HGRN_BLOCK
    halves = []
    h = HGRN_BLOCK
    while h < group:
        halves.append(h)
        h *= 2
    assert halves, "group must span at least two blocks"
    tok = np.arange(group)

    def span(b0, b1):
        return ((tok >= b0 * HGRN_BLOCK) & (tok < b1 * HGRN_BLOCK)).astype(np.float32)

    zero = np.zeros(group, np.float32)
    rows, valid, index = [], [], {}
    for h in halves:
        hb = h // HGRN_BLOCK
        krows, kval, qrows, qval = [], [], [], []
        for b in range(nb):
            bmid = (b // (2 * hb)) * 2 * hb + hb
            first = b < bmid
            krows.append(span(b, bmid) if first else zero)
            kval.append(1.0 if first else 0.0)
            qrows.append(zero if first else span(bmid, b))
            qval.append(0.0 if first else 1.0)
        index[("k", h)] = len(rows)
        rows += krows
        valid += kval
        if hb > 1:
            index[("q", h)] = len(rows)
            rows += qrows
            valid += qval
    index["q_in"] = len(rows)
    rows += [span(0, b) for b in range(nb)]
    valid += [1.0] * nb
    index["k_out"] = len(rows)
    rows += [span(b, nb) for b in range(nb)]
    valid += [1.0] * nb
    index["decay"] = len(rows)
    rows.append(span(0, nb))
    valid.append(1.0)
    nx = -(-len(rows) // BF16_SUBLANES) * BF16_SUBLANES
    while len(rows) < nx:
        rows.append(zero)
        valid.append(0.0)

    local = ((tok[:, None] // HGRN_BLOCK == tok[None, :] // HGRN_BLOCK) & (tok[None, :] <= tok[:, None]))
    lhs = np.concatenate([local.astype(np.float32), np.stack(rows)], axis=0)
    lhs = np.concatenate([lhs, lhs], axis=1)
    valid = np.broadcast_to(np.asarray(valid, np.float32)[:, None], (nx, width)).copy()

    rb, cb = tok[:, None] // HGRN_BLOCK, tok[None, :] // HGRN_BLOCK
    cls = np.full((group, group), len(halves), np.int32)
    for li, h in enumerate(halves[:-1]):
        hb = h // HGRN_BLOCK
        own = (rb // (2 * hb) == cb // (2 * hb)) & ((rb // hb) % 2 == 1) & ((cb // hb) % 2 == 0)
        cls[own] = li + 1
    cls[local] = 0
    return lhs, valid, cls, index, halves


def _hgrn_kernel(*refs, layer, group, seqs, seq_rows, index, halves, n_cast):
    q_ref, f_ref, i_ref, og_ref, lbl_ref, nrm_ref, s0_ref, lhs_ref, valid_ref, cls_ref = refs[:10]
    cast_in = refs[10:10 + n_cast]
    y_ref, s_ref = refs[10 + n_cast:12 + n_cast]
    cast_out = refs[12 + n_cast:12 + 2 * n_cast]
    st_ref, fc_ref = refs[12 + 2 * n_cast:]
    chunk = pl.program_id(2)
    width = HEADS_PER_STEP * HEAD_DIM
    nb = group // HGRN_BLOCK
    _cast_blocks(cast_in, cast_out)

    @pl.when(chunk == 0)
    def _():
        for j in range(seqs):
            for hh in range(HEADS_PER_STEP):
                st_ref[j, hh] = s0_ref[j, hh].T

    logits = lbl_ref[...]
    e = jnp.exp(logits - jnp.max(logits, axis=0, keepdims=True))
    lb = jnp.sum(e[:layer + 1], axis=0, keepdims=True) / jnp.sum(e, axis=0, keepdims=True)
    lb1 = 1.0 - lb
    gain = nrm_ref[...]
    lhs = lhs_ref[...]
    valid = valid_ref[...]
    cls = cls_ref[...]
    masks = [cls == ci for ci in range(len(halves))]

    n_items = seqs * (seq_rows // group)

    def rows_of(item):
        j, g = divmod(item, seq_rows // group)
        return j, slice(j * seq_rows + g * group, j * seq_rows + (g + 1) * group)

    def front(item):
        j, rs = rows_of(item)
        q = q_ref[0, rs, :].astype(F32)
        fz = f_ref[0, rs, :].astype(F32)
        v = i_ref[0, rs, :]
        og = og_ref[0, rs, :].astype(F32)

        f = lb + lb1 * jax.nn.sigmoid(fz)
        lg = jnp.log(f) * INV_LN2
        k = 1.0 - f
        qf = q * jax.nn.sigmoid(q)

        top = pltpu.bitcast(pltpu.bitcast(lg, jnp.uint32) & jnp.uint32(BF16_BITS_OF_F32), F32)
        pieces = jnp.concatenate([top.astype(BF16), (lg - top).astype(BF16)], axis=0)
        sums = _dot(lhs, pieces)
        e_loc = sums[:group]
        fc_ref[item] = jnp.exp2(sums[group:]) * valid

        def expand(name):
            base = index[name]
            tiles = []
            for b in range(nb):
                tile = jnp.broadcast_to(fc_ref[item, pl.ds(base + b, 1), :], (F32_SUBLANES, width))
                tiles.append(jnp.concatenate([tile] * (HGRN_BLOCK // F32_SUBLANES), axis=0).astype(BF16))
            return jnp.concatenate(tiles, axis=0)

        qd_b = (qf * jnp.exp2(e_loc)).astype(BF16)
        kd_b = (k * jnp.exp2(-e_loc)).astype(BF16)
        q_lv = [qd_b if h == HGRN_BLOCK else qd_b * expand(("q", h)) for h in halves]
        k_lv = [kd_b * expand(("k", h)) for h in halves]
        q_in = qd_b * expand("q_in")
        k_out = kd_b * expand("k_out")
        sog = (og * jax.nn.sigmoid(og)) * gain
        return qd_b, kd_b, q_lv, k_lv, q_in, k_out, v, sog

    def mid(ops):
        qd_b, kd_b, q_lv, k_lv, q_in, k_out, v, sog = ops
        a_heads = []
        for hh in range(HEADS_PER_STEP):
            sl = slice(hh * HEAD_DIM, (hh + 1) * HEAD_DIM)
            if group % 128 == 0:
                both = _dot_nt(qd_b[:, sl], jnp.concatenate([kd_b[:, sl], k_lv[0][:, sl]], axis=0))
                parts = [both[:, :group], both[:, group:]]
            else:
                parts = [_dot_nt(qd_b[:, sl], kd_b[:, sl]), _dot_nt(qd_b[:, sl], k_lv[0][:, sl])]
            parts += [_dot_nt(ql[:, sl], kl[:, sl]) for ql, kl in zip(q_lv[1:], k_lv[1:])]
            a = parts[-1]
            for ci in reversed(range(len(halves))):
                a = jnp.where(masks[ci], parts[ci], a)
            a_heads.append(a.astype(BF16))
        return a_heads, q_in, k_out, v, sog

    def back(item, ops):
        a_heads, q_in, k_out, v, sog = ops
        j, rs = rows_of(item)
        decay = fc_ref[item, pl.ds(index["decay"], 1), :]
        outs = []
        for hh in range(HEADS_PER_STEP):
            sl = slice(hh * HEAD_DIM, (hh + 1) * HEAD_DIM)
            st = st_ref[j, hh]
            o = _dot(a_heads[hh], v[:, sl]) + _dot_nt(q_in[:, sl], st.astype(BF16))
            st_ref[j, hh] = st * decay[:, sl] + _dot_tn(v[:, sl], k_out[:, sl])
            outs.append(o * lax.rsqrt(jnp.mean(o * o, axis=-1, keepdims=True) + NORM_EPS))
        y_ref[0, rs, :] = (jnp.concatenate(outs, axis=1) * sog).astype(y_ref.dtype)

    fronts, mids = {}, {}
    for t in range(n_items + 2):
        if t < n_items:
            fronts[t] = front(t)
        if 0 <= t - 1 < n_items:
            mids[t - 1] = mid(fronts.pop(t - 1))
        if 0 <= t - 2 < n_items:
            back(t - 2, mids.pop(t - 2))

    @pl.when(chunk == pl.num_programs(2) - 1)
    def _():
        for j in range(seqs):
            for hh in range(HEADS_PER_STEP):
                s_ref[j, hh] = st_ref[j, hh].T


def _hgrn(p, lb_logits, hgrn_norm, s0, *, first_group, layer, batch, seqs, tc, group, cast=()):
    _, m, width = p.shape
    npairs = s0.shape[1] // HEADS_PER_STEP
    t = m // batch
    nc = t // tc
    assert seqs == 1 or nc == 1
    rows = seqs * tc
    lhs, valid, cls, index, halves = _hgrn_tables(group, width)
    row = lambda b, c: b * nc + c
    part = lambda k: pl.BlockSpec((1, rows, width), lambda b, hp, c: (first_group + k * npairs + hp, row(b, c), 0))
    state_spec = pl.BlockSpec((seqs, HEADS_PER_STEP, HEAD_DIM, HEAD_DIM), lambda b, hp, c: (b, hp, 0, 0))
    const = lambda a: pl.BlockSpec(a.shape, lambda b, hp, c: (0, 0))
    n_items = seqs * (tc // group)
    grid = (batch // seqs, npairs, nc)
    cast_in_specs, cast_out_specs, cast_shapes = _cast_streams(
        cast, grid[0] * grid[1] * grid[2], lambda b, hp, c: (b * npairs + hp) * nc + c)
    outs = pl.pallas_call(
        functools.partial(_hgrn_kernel, layer=layer, group=group, seqs=seqs, seq_rows=tc, index=index,
                          halves=halves, n_cast=len(cast)),
        grid=grid,
        in_specs=[
            part(0), part(1), part(2), part(3),
            pl.BlockSpec((lb_logits.shape[0], width), lambda b, hp, c: (0, hp)),
            pl.BlockSpec((1, width), lambda b, hp, c: (0, hp)),
            state_spec, const(lhs), const(valid), const(cls),
        ] + cast_in_specs,
        out_specs=[
            pl.BlockSpec((1, rows, width), lambda b, hp, c: (hp, row(b, c), 0)),
            state_spec,
        ] + cast_out_specs,
        out_shape=[
            jax.ShapeDtypeStruct((npairs, m, width), BF16),
            jax.ShapeDtypeStruct(s0.shape, F32),
        ] + cast_shapes,
        scratch_shapes=[
            pltpu.VMEM((seqs, HEADS_PER_STEP, HEAD_DIM, HEAD_DIM), F32),
            pltpu.VMEM((n_items, valid.shape[0], width), F32),
        ],
        compiler_params=_params(("arbitrary", "arbitrary", "arbitrary")),
        name="hgrn2",
    )(p, p, p, p, lb_logits, hgrn_norm, s0, jnp.asarray(lhs, BF16), jnp.asarray(valid), jnp.asarray(cls), *cast)
    return outs[0], outs[1], outs[2:]


def _aligned_blocks(first, count):
    if first % count == 0:
        return [(first // count, count)]
    for head in range(1, count):
        if first % head == 0 and (first + head) % (count - head) == 0:
            return [(first // head, head), ((first + head) // (count - head), count - head)]
    raise ValueError(f"groups [{first}, {first + count}) cannot be covered by two aligned blocks")


def _merge_kernel(*refs, tiles_per_seq, n_gate_blocks):
    bcv_ref, c0_ref, cw_ref, yh_ref = refs[:4]
    gate_refs = refs[4:4 + n_gate_blocks]
    wbc_ref, wbh_ref, o_ref, tail_ref, ubuf = refs[4 + n_gate_blocks:]

    @pl.when(pl.program_id(0) % tiles_per_seq == 0)
    def _():
        tail_ref[...] = c0_ref[...]

    seqs = tail_ref.shape[0]
    tm, d = o_ref.shape
    t = tm // seqs
    ng = bcv_ref.shape[0] // 3

    def wide(first, rs):
        return jnp.concatenate([bcv_ref[first + g, rs, :] for g in range(ng)], axis=1).astype(F32)

    def branch(groups, w_ref):
        acc = _dot(groups[0], w_ref[0])
        for g in range(1, len(groups)):
            acc = acc + _dot(groups[g], w_ref[g])
        return acc

    bh = branch([yh_ref[g] for g in range(yh_ref.shape[0])], wbh_ref)
    cw = cw_ref[...]
    ys = []
    for s in range(seqs):
        rs = slice(s * t, (s + 1) * t)
        y, tail = _conv_rows(wide(0, rs), wide(ng, rs), wide(2 * ng, rs), tail_ref[s], cw, ubuf)
        tail_ref[s] = tail
        ys.append(y.astype(BF16))
    yc = jnp.concatenate(ys, axis=0)
    bc = branch([yc[:, g * LANE_GROUP:(g + 1) * LANE_GROUP] for g in range(ng)], wbc_ref)
    gates = jax.nn.sigmoid(
        jnp.concatenate([r[g] for r in gate_refs for g in range(r.shape[0])], axis=1).astype(F32))
    o_ref[...] = (gates[:, :d] * bc + gates[:, d:] * bh).astype(o_ref.dtype)


def _out_kernel(m_ref, x_ref, wo_ref, o_ref):
    o_ref[...] = x_ref[...] + _dot(m_ref[...], wo_ref[...])


def _merge(p, yh, x, conv0, conv_w, wbc, wbh, wo, *, conv_group, gate_group, batch, tm, tm_out):
    m, d = x.shape
    gw = p.shape[-1]
    dc = conv0.shape[-1]
    (conv_block, n_conv), = _aligned_blocks(conv_group, 3 * dc // gw)
    gate_blocks = _aligned_blocks(gate_group, 2 * d // gw)
    t = m // batch
    seqs_per_tile, tiles_per_seq = max(1, tm // t), max(1, t // tm)
    const = lambda shape: pl.BlockSpec(shape, lambda i: (0,) * len(shape), pipeline_mode=pl.Buffered(1))
    groups = lambda block, size: pl.BlockSpec((size, tm, gw), lambda i: (block, i, 0))
    tail_spec = pl.BlockSpec((seqs_per_tile, CONV_WIDTH - 1, dc), lambda i: (i // tiles_per_seq, 0, 0))
    merged, tails = pl.pallas_call(
        functools.partial(_merge_kernel, tiles_per_seq=tiles_per_seq, n_gate_blocks=len(gate_blocks)),
        grid=(m // tm,),
        in_specs=[
            groups(conv_block, n_conv), tail_spec,
            pl.BlockSpec((CONV_WIDTH, dc), lambda i: (0, 0)),
            pl.BlockSpec((yh.shape[0], tm, gw), lambda i: (0, i, 0)),
            *[groups(block, size) for block, size in gate_blocks],
            const(wbc.shape), const(wbh.shape),
        ],
        out_specs=[pl.BlockSpec((tm, d), lambda i: (i, 0)), tail_spec],
        out_shape=[jax.ShapeDtypeStruct((m, d), BF16), jax.ShapeDtypeStruct(conv0.shape, F32)],
        scratch_shapes=[pltpu.VMEM((tm // seqs_per_tile + F32_SUBLANES, dc), F32)],
        compiler_params=_params(("arbitrary",)),
        name="conv_merge",
    )(p, conv0, conv_w, yh, *[p] * len(gate_blocks), wbc, wbh)
    row_tile = pl.BlockSpec((tm_out, d), lambda i: (i, 0))
    x2 = pl.pallas_call(
        _out_kernel,
        grid=(m // tm_out,),
        in_specs=[row_tile, row_tile, const(wo.shape)],
        out_specs=row_tile,
        out_shape=jax.ShapeDtypeStruct((m, d), F32),
        compiler_params=_params(("arbitrary",)),
        name="out_proj",
    )(merged, x, wo)
    return x2, tails


MAX_ROW_TILE = 1024
FFN_CHUNK = 512
PROJ_CHUNK = 2560
MERGE_ROW_TILE = 512
HGRN_GROUP = 128
HGRN_CHUNK = 4096


def _plan_tiles(batch, t):
    m = batch * t
    tiles = dict(tm_ffn=min(MAX_ROW_TILE, m), tf=FFN_CHUNK, tm_proj=min(MAX_ROW_TILE, m), tn_proj=PROJ_CHUNK,
                 tm_merge=min(MERGE_ROW_TILE, m), tm_out=min(MAX_ROW_TILE, m))
    if t >= HGRN_GROUP:
        tiles.update(seqs_hgrn=1, tc_hgrn=min(HGRN_CHUNK, t), group=HGRN_GROUP)
    else:
        tiles.update(seqs_hgrn=batch, tc_hgrn=t, group=t)
    return tiles


def _mix_and_ffn2(x1, conv0, s0, w, later, *, batch, tm_ffn, tf, tm_proj, tn_proj, tm_merge, tm_out, seqs_hgrn, tc_hgrn,
                  group):
    d = x1.shape[1]
    groups = w["in_groups"]
    p = _proj(x1, w["norm_mix"], w["in"], group_width=LANE_GROUP, tm=tm_proj, tn=tn_proj)
    pending = [a for a in later if a.dtype != BF16]
    yh, s_new, done = _hgrn(p, w["lb_logits"], w["hgrn_norm"], s0, first_group=groups["hgrn"], layer=0, batch=batch,
                            seqs=seqs_hgrn, tc=tc_hgrn, group=group, cast=pending)
    later = tuple(done) if pending else later
    wbc, wbh, wo, w2g, w2u, w2d = later
    grouped = lambda a: a.reshape(a.shape[0] // LANE_GROUP, LANE_GROUP, d)
    x2, conv_new = _merge(p, yh, x1, conv0, w["conv_w"], grouped(wbc), grouped(wbh), wo, conv_group=groups["conv"],
                          gate_group=groups["gate"], batch=batch, tm=tm_merge, tm_out=tm_out)
    y, _ = _ffn(x2, w["norm_ffn2"], w2g, w2u, w2d, w["norm_final"], final_norm=True, tm=tm_ffn, tf=tf)
    return y, conv_new, s_new, later


def _forward(x_prompt, x_sample, cache_conv, state_hgrn, norm_ffn1, w_ffn1_gate, w_ffn1_up, w_ffn1_down, norm_mix, w_in,
             conv_w, hgrn_lb_logits, hgrn_norm, w_br_conv, w_br_hgrn, w_out, norm_ffn2, w_ffn2_gate, w_ffn2_up,
             w_ffn2_down, norm_final, *, prompt_tiles=None, sample_tiles=None):
    assert w_in.shape[0] == 1, "single-layer trunk"
    prompt_tiles = prompt_tiles or _plan_tiles(*x_prompt.shape[:2])
    sample_tiles = sample_tiles or _plan_tiles(*x_sample.shape[:2])
    heads, dk, dv = state_hgrn.shape[-3:]
    assert dk == HEAD_DIM and dv == HEAD_DIM and heads % HEADS_PER_STEP == 0
    d = x_prompt.shape[-1]
    dc, dh = cache_conv.shape[-1], heads * dk
    bf = lambda a: a.astype(BF16)
    gf = norm_final.reshape(1, d)
    ffn1 = (norm_ffn1, bf(w_ffn1_gate[0]), bf(w_ffn1_up[0]), bf(w_ffn1_down[0]), gf)

    bp, tp, _ = x_prompt.shape
    x1p, (win,) = _ffn(x_prompt.reshape(bp * tp, d), *ffn1, final_norm=False, tm=prompt_tiles["tm_ffn"],
                       tf=prompt_tiles["tf"], cast=(w_in,))
    n_conv = 3 * dc
    w = dict(
        norm_mix=norm_mix, **{"in": win},
        in_groups=dict(conv=0, hgrn=n_conv // LANE_GROUP, gate=(n_conv + 4 * dh) // LANE_GROUP),
        conv_w=conv_w[0], lb_logits=hgrn_lb_logits, hgrn_norm=hgrn_norm, norm_ffn2=norm_ffn2, norm_final=gf,
    )
    later = (w_br_conv, w_br_hgrn, w_out, w_ffn2_gate, w_ffn2_up, w_ffn2_down)
    zero_conv = jnp.zeros((bp,) + cache_conv.shape[2:], F32)
    zero_hgrn = jnp.zeros((bp,) + state_hgrn.shape[2:], F32)
    yp, cp, sp, later = _mix_and_ffn2(x1p, zero_conv, zero_hgrn, w, later, batch=bp, **prompt_tiles)

    bs, ts, _ = x_sample.shape
    x1s, _ = _ffn(x_sample.reshape(bs * ts, d), *ffn1, final_norm=False, tm=sample_tiles["tm_ffn"], tf=sample_tiles["tf"])
    ys, cs, ss, _ = _mix_and_ffn2(x1s, cache_conv[0], state_hgrn[0], w, later, batch=bs, **sample_tiles)
    return yp.reshape(bp, tp, d), ys.reshape(bs, ts, d), cp[None], sp[None], cs[None], ss[None]


def kernel(x_prompt, x_sample, cache_conv, state_hgrn, norm_ffn1, w_ffn1_gate, w_ffn1_up, w_ffn1_down, norm_mix, w_in, conv_w, hgrn_lb_logits, hgrn_norm, w_br_conv, w_br_hgrn, w_out, norm_ffn2, w_ffn2_gate, w_ffn2_up, w_ffn2_down, norm_final):
    return _forward(
        x_prompt, x_sample, cache_conv, state_hgrn, norm_ffn1, w_ffn1_gate, w_ffn1_up, w_ffn1_down, norm_mix, w_in,
        conv_w, hgrn_lb_logits, hgrn_norm, w_br_conv, w_br_hgrn, w_out, norm_ffn2, w_ffn2_gate, w_ffn2_up, w_ffn2_down,
        norm_final)
```

```python
import functools

import numpy as np
import jax
import jax.numpy as jnp
from jax import lax
from jax.experimental import pallas as pl
from jax.experimental.pallas import tpu as pltpu

NORM_EPS = 1e-6
INV_LN2 = 1.4426950408889634
CONV_WIDTH = 3
HGRN_BLOCK = 16
HEAD_DIM = 128
HEADS_PER_STEP = 2
LANE_GROUP = HEADS_PER_STEP * HEAD_DIM
LANES = 128
F32_SUBLANES = 8
BF16_SUBLANES = 16
BF16_BITS_OF_F32 = 0xFFFF0000
VMEM_LIMIT_BYTES = 56 * 1024 * 1024

F32 = jnp.float32
BF16 = jnp.bfloat16


def _params(semantics):
    return pltpu.CompilerParams(dimension_semantics=semantics, vmem_limit_bytes=VMEM_LIMIT_BYTES)


def _rms(x, gain):
    r = lax.rsqrt(jnp.mean(x * x, axis=-1, keepdims=True) + NORM_EPS)
    return x * r * gain


def _dot(a, b):
    return jnp.dot(a, b, preferred_element_type=F32)


def _dot_nt(a, b):
    return lax.dot_general(a, b, (((1,), (1,)), ((), ())), preferred_element_type=F32)


def _dot_tn(a, b):
    return lax.dot_general(a, b, (((0,), (0,)), ((), ())), preferred_element_type=F32)


def _cast_streams(arrays, n_steps, step_of):
    in_specs, out_specs, shapes = [], [], []
    for a in arrays:
        _, rows, cols = a.shape
        cuts = [(nr, nc) for nr in range(1, rows // BF16_SUBLANES + 1) if rows % (nr * BF16_SUBLANES) == 0
                for nc in range(1, cols // LANES + 1) if cols % (nc * LANES) == 0
                if nr * nc <= n_steps]
        nr, nc = max(cuts, key=lambda c: (c[0] * c[1], c[0]))

        def block(*ids, nr=nr, nc=nc):
            s = jnp.minimum(step_of(*ids), nr * nc - 1)
            return s // nc, s % nc

        in_specs.append(pl.BlockSpec((None, rows // nr, cols // nc), lambda *ids, block=block: (0,) + block(*ids)))
        out_specs.append(pl.BlockSpec((rows // nr, cols // nc), block))
        shapes.append(jax.ShapeDtypeStruct((rows, cols), BF16))
    return in_specs, out_specs, shapes


def _cast_blocks(in_refs, out_refs):
    for src_ref, dst_ref in zip(in_refs, out_refs):
        dst_ref[...] = src_ref[...].astype(BF16)


def _ffn_kernel(*refs, final_norm, n_cast):
    x_ref, g_ref, wg_ref, wu_ref, wd_ref, gf_ref = refs[:6]
    cast_in = refs[6:6 + n_cast]
    o_ref = refs[6 + n_cast]
    cast_out = refs[7 + n_cast:7 + 2 * n_cast]
    h_ref = refs[7 + 2 * n_cast]
    f = pl.program_id(1)

    @pl.when(f == 0)
    def _():
        x = x_ref[...]
        h_ref[...] = _rms(x, g_ref[...]).astype(BF16)
        o_ref[...] = x

    h = h_ref[...]
    tf = wg_ref.shape[1]
    n_sub = 2
    cols = [slice(s * tf // n_sub, (s + 1) * tf // n_sub) for s in range(n_sub)]

    def up(cs):
        g = _dot(h, wg_ref[:, cs])
        u = _dot(h, wu_ref[:, cs])
        return (g * jax.nn.sigmoid(g) * u * 0.5).astype(BF16)

    acc = None
    pending = up(cols[0])
    for s, cs in enumerate(cols):
        nxt = up(cols[s + 1]) if s + 1 < n_sub else None
        part = _dot(pending, wd_ref[cs, :])
        acc = part if acc is None else acc + part
        pending = nxt
    o_ref[...] += acc

    _cast_blocks(cast_in, cast_out)

    if final_norm:
        @pl.when(f == pl.num_programs(1) - 1)
        def _():
            o_ref[...] = _rms(o_ref[...], gf_ref[...])


def _ffn(x, gain, wg, wu, wd, final_gain, *, final_norm, tm, tf, cast=()):
    m, d = x.shape
    dff = wg.shape[1]
    n_i, n_f = m // tm, dff // tf
    cast_in_specs, cast_out_specs, cast_shapes = _cast_streams(cast, n_i * n_f, lambda i, f: i * n_f + f)
    outs = pl.pallas_call(
        functools.partial(_ffn_kernel, final_norm=final_norm, n_cast=len(cast)),
        grid=(n_i, n_f),
        in_specs=[
            pl.BlockSpec((tm, d), lambda i, f: (i, 0)),
            pl.BlockSpec((1, d), lambda i, f: (0, 0)),
            pl.BlockSpec((d, tf), lambda i, f: (0, f)),
            pl.BlockSpec((d, tf), lambda i, f: (0, f)),
            pl.BlockSpec((tf, d), lambda i, f: (f, 0)),
            pl.BlockSpec((1, d), lambda i, f: (0, 0)),
        ] + cast_in_specs,
        out_specs=[pl.BlockSpec((tm, d), lambda i, f: (i, 0))] + cast_out_specs,
        out_shape=[jax.ShapeDtypeStruct((m, d), F32)] + cast_shapes,
        scratch_shapes=[pltpu.VMEM((tm, d), BF16)],
        compiler_params=_params(("arbitrary", "arbitrary")),
        name="ffn_final" if final_norm else "ffn",
    )(x, gain, wg, wu, wd, final_gain, *cast)
    return outs[0], outs[1:]


def _proj_kernel(x_ref, g_ref, w_ref, o_ref, h_ref, *, group_width):
    @pl.when(pl.program_id(1) == 0)
    def _():
        h_ref[...] = _rms(x_ref[...], g_ref[...]).astype(BF16)

    res = _dot(h_ref[...], w_ref[...]).astype(o_ref.dtype)
    for k in range(o_ref.shape[0]):
        o_ref[k] = res[:, k * group_width:(k + 1) * group_width]


def _proj(x, gain, w, *, group_width, tm, tn):
    m, d = x.shape
    n = w.shape[1]
    gpb = tn // group_width
    return pl.pallas_call(
        functools.partial(_proj_kernel, group_width=group_width),
        grid=(m // tm, n // tn),
        in_specs=[
            pl.BlockSpec((tm, d), lambda i, j: (i, 0)),
            pl.BlockSpec((1, d), lambda i, j: (0, 0)),
            pl.BlockSpec((d, tn), lambda i, j: (0, j)),
        ],
        out_specs=pl.BlockSpec((gpb, tm, group_width), lambda i, j: (j, i, 0)),
        out_shape=jax.ShapeDtypeStruct((n // group_width, m, group_width), BF16),
        scratch_shapes=[pltpu.VMEM((tm, d), BF16)],
        compiler_params=_params(("arbitrary", "arbitrary")),
        name="in_proj",
    )(x, gain, w)


def _conv_rows(b, c, v, prev, w, ubuf):
    t = b.shape[0]
    pad = F32_SUBLANES
    u = c * v
    ubuf[pad - (CONV_WIDTH - 1):pad, :] = prev
    ubuf[pad:pad + t, :] = u
    acc = w[CONV_WIDTH - 1:CONV_WIDTH] * u
    for j in range(CONV_WIDTH - 1):
        shift = CONV_WIDTH - 1 - j
        acc = acc + w[j:j + 1] * ubuf[pad - shift:pad - shift + t, :]
    return b * acc, u[t - (CONV_WIDTH - 1):, :]


def _hgrn_tables(group, width):
    nb = group // HGRN_BLOCK
    halves = []
    h = HGRN_BLOCK
    while h < group:
        halves.append(h)
        h *= 2
    assert halves, "group must span at least two blocks"
    tok = np.arange(group)

    def span(b0, b1):
        return ((tok >= b0 * HGRN_BLOCK) & (tok < b1 * HGRN_BLOCK)).astype(np.float32)

    zero = np.zeros(group, np.float32)
    rows, valid, index = [], [], {}
    for h in halves:
        hb = h // HGRN_BLOCK
        krows, kval, qrows, qval = [], [], [], []
        for b in range(nb):
            bmid = (b // (2 * hb)) * 2 * hb + hb
            first = b < bmid
            krows.append(span(b, bmid) if first else zero)
            kval.append(1.0 if first else 0.0)
            qrows.append(zero if first else span(bmid, b))
            qval.append(0.0 if first else 1.0)
        index[("k", h)] = len(rows)
        rows += krows
        valid += kval
        if hb > 1:
            index[("q", h)] = len(rows)
            rows += qrows
            valid += qval
    index["q_in"] = len(rows)
    rows += [span(0, b) for b in range(nb)]
    valid += [1.0] * nb
    index["k_out"] = len(rows)
    rows += [span(b, nb) for b in range(nb)]
    valid += [1.0] * nb
    index["decay"] = len(rows)
    rows.append(span(0, nb))
    valid.append(1.0)
    nx = -(-len(rows) // BF16_SUBLANES) * BF16_SUBLANES
    while len(rows) < nx:
        rows.append(zero)
        valid.append(0.0)

    local = ((tok[:, None] // HGRN_BLOCK == tok[None, :] // HGRN_BLOCK) & (tok[None, :] <= tok[:, None]))
    lhs = np.concatenate([local.astype(np.float32), np.stack(rows)], axis=0)
    lhs = np.concatenate([lhs, lhs], axis=1)
    valid = np.broadcast_to(np.asarray(valid, np.float32)[:, None], (nx, width)).copy()

    rb, cb = tok[:, None] // HGRN_BLOCK, tok[None, :] // HGRN_BLOCK
    cls = np.full((group, group), len(halves), np.int32)
    for li, h in enumerate(halves[:-1]):
        hb = h // HGRN_BLOCK
        own = (rb // (2 * hb) == cb // (2 * hb)) & ((rb // hb) % 2 == 1) & ((cb // hb) % 2 == 0)
        cls[own] = li + 1
    cls[local] = 0
    return lhs, valid, cls, index, halves


def _hgrn_kernel(*refs, layer, group, seqs, seq_rows, index, halves, n_cast):
    q_ref, f_ref, i_ref, og_ref, lbl_ref, nrm_ref, s0_ref, lhs_ref, valid_ref, cls_ref = refs[:10]
    cast_in = refs[10:10 + n_cast]
    y_ref, s_ref = refs[10 + n_cast:12 + n_cast]
    cast_out = refs[12 + n_cast:12 + 2 * n_cast]
    st_ref, fc_ref = refs[12 + 2 * n_cast:]
    chunk = pl.program_id(2)
    width = HEADS_PER_STEP * HEAD_DIM
    nb = group // HGRN_BLOCK
    _cast_blocks(cast_in, cast_out)

    @pl.when(chunk == 0)
    def _():
        for j in range(seqs):
            for hh in range(HEADS_PER_STEP):
                st_ref[j, hh] = s0_ref[j, hh].T

    logits = lbl_ref[...]
    e = jnp.exp(logits - jnp.max(logits, axis=0, keepdims=True))
    lb = jnp.sum(e[:layer + 1], axis=0, keepdims=True) / jnp.sum(e, axis=0, keepdims=True)
    lb1 = 1.0 - lb
    gain = nrm_ref[...]
    lhs = lhs_ref[...]
    valid = valid_ref[...]
    cls = cls_ref[...]
    masks = [cls == ci for ci in range(len(halves))]

    n_items = seqs * (seq_rows // group)

    def rows_of(item):
        j, g = divmod(item, seq_rows // group)
        return j, slice(j * seq_rows + g * group, j * seq_rows + (g + 1) * group)

    def front(item):
        j, rs = rows_of(item)
        q = q_ref[0, rs, :].astype(F32)
        fz = f_ref[0, rs, :].astype(F32)
        v = i_ref[0, rs, :]
        og = og_ref[0, rs, :].astype(F32)

        f = lb + lb1 * jax.nn.sigmoid(fz)
        lg = jnp.log(f) * INV_LN2
        k = 1.0 - f
        qf = q * jax.nn.sigmoid(q)

        top = pltpu.bitcast(pltpu.bitcast(lg, jnp.uint32) & jnp.uint32(BF16_BITS_OF_F32), F32)
        pieces = jnp.concatenate([top.astype(BF16), (lg - top).astype(BF16)], axis=0)
        sums = _dot(lhs, pieces)
        e_loc = sums[:group]
        fc_ref[item] = jnp.exp2(sums[group:]) * valid

        def expand(name):
            base = index[name]
            tiles = []
            for b in range(nb):
                tile = jnp.broadcast_to(fc_ref[item, pl.ds(base + b, 1), :], (F32_SUBLANES, width))
                tiles.append(jnp.concatenate([tile] * (HGRN_BLOCK // F32_SUBLANES), axis=0).astype(BF16))
            return jnp.concatenate(tiles, axis=0)

        qd_b = (qf * jnp.exp2(e_loc)).astype(BF16)
        kd_b = (k * jnp.exp2(-e_loc)).astype(BF16)
        q_lv = [qd_b if h == HGRN_BLOCK else qd_b * expand(("q", h)) for h in halves]
        k_lv = [kd_b * expand(("k", h)) for h in halves]
        q_in = qd_b * expand("q_in")
        k_out = kd_b * expand("k_out")
        sog = (og * jax.nn.sigmoid(og)) * gain
        return qd_b, kd_b, q_lv, k_lv, q_in, k_out, v, sog

    def mid(ops):
        qd_b, kd_b, q_lv, k_lv, q_in, k_out, v, sog = ops
        a_heads = []
        for hh in range(HEADS_PER_STEP):
            sl = slice(hh * HEAD_DIM, (hh + 1) * HEAD_DIM)
            if group % 128 == 0:
                both = _dot_nt(qd_b[:, sl], jnp.concatenate([kd_b[:, sl], k_lv[0][:, sl]], axis=0))
                parts = [both[:, :group], both[:, group:]]
            else:
                parts = [_dot_nt(qd_b[:, sl], kd_b[:, sl]), _dot_nt(qd_b[:, sl], k_lv[0][:, sl])]
            parts += [_dot_nt(ql[:, sl], kl[:, sl]) for ql, kl in zip(q_lv[1:], k_lv[1:])]
            a = parts[-1]
            for ci in reversed(range(len(halves))):
                a = jnp.where(masks[ci], parts[ci], a)
            a_heads.append(a.astype(BF16))
        return a_heads, q_in, k_out, v, sog

    def back(item, ops):
        a_heads, q_in, k_out, v, sog = ops
        j, rs = rows_of(item)
        decay = fc_ref[item, pl.ds(index["decay"], 1), :]
        outs = []
        for hh in range(HEADS_PER_STEP):
            sl = slice(hh * HEAD_DIM, (hh + 1) * HEAD_DIM)
            st = st_ref[j, hh]
            o = _dot(a_heads[hh], v[:, sl]) + _dot_nt(q_in[:, sl], st.astype(BF16))
            st_ref[j, hh] = st * decay[:, sl] + _dot_tn(v[:, sl], k_out[:, sl])
            outs.append(o * lax.rsqrt(jnp.mean(o * o, axis=-1, keepdims=True) + NORM_EPS))
        y_ref[0, rs, :] = (jnp.concatenate(outs, axis=1) * sog).astype(y_ref.dtype)

    fronts, mids = {}, {}
    for t in range(n_items + 2):
        if t < n_items:
            fronts[t] = front(t)
        if 0 <= t - 1 < n_items:
            mids[t - 1] = mid(fronts.pop(t - 1))
        if 0 <= t - 2 < n_items:
            back(t - 2, mids.pop(t - 2))

    @pl.when(chunk == pl.num_programs(2) - 1)
    def _():
        for j in range(seqs):
            for hh in range(HEADS_PER_STEP):
                s_ref[j, hh] = st_ref[j, hh].T


def _hgrn(p, lb_logits, hgrn_norm, s0, *, first_group, layer, batch, seqs, tc, group, cast=()):
    _, m, width = p.shape
    npairs = s0.shape[1] // HEADS_PER_STEP
    t = m // batch
    nc = t // tc
    assert seqs == 1 or nc == 1
    rows = seqs * tc
    lhs, valid, cls, index, halves = _hgrn_tables(group, width)
    row = lambda b, c: b * nc + c
    part = lambda k: pl.BlockSpec((1, rows, width), lambda b, hp, c: (first_group + k * npairs + hp, row(b, c), 0))
    state_spec = pl.BlockSpec((seqs, HEADS_PER_STEP, HEAD_DIM, HEAD_DIM), lambda b, hp, c: (b, hp, 0, 0))
    const = lambda a: pl.BlockSpec(a.shape, lambda b, hp, c: (0, 0))
    n_items = seqs * (tc // group)
    grid = (batch // seqs, npairs, nc)
    cast_in_specs, cast_out_specs, cast_shapes = _cast_streams(
        cast, grid[0] * grid[1] * grid[2], lambda b, hp, c: (b * npairs + hp) * nc + c)
    outs = pl.pallas_call(
        functools.partial(_hgrn_kernel, layer=layer, group=group, seqs=seqs, seq_rows=tc, index=index,
                          halves=halves, n_cast=len(cast)),
        grid=grid,
        in_specs=[
            part(0), part(1), part(2), part(3),
            pl.BlockSpec((lb_logits.shape[0], width), lambda b, hp, c: (0, hp)),
            pl.BlockSpec((1, width), lambda b, hp, c: (0, hp)),
            state_spec, const(lhs), const(valid), const(cls),
        ] + cast_in_specs,
        out_specs=[
            pl.BlockSpec((1, rows, width), lambda b, hp, c: (hp, row(b, c), 0)),
            state_spec,
        ] + cast_out_specs,
        out_shape=[
            jax.ShapeDtypeStruct((npairs, m, width), BF16),
            jax.ShapeDtypeStruct(s0.shape, F32),
        ] + cast_shapes,
        scratch_shapes=[
            pltpu.VMEM((seqs, HEADS_PER_STEP, HEAD_DIM, HEAD_DIM), F32),
            pltpu.VMEM((n_items, valid.shape[0], width), F32),
        ],
        compiler_params=_params(("arbitrary", "arbitrary", "arbitrary")),
        name="hgrn2",
    )(p, p, p, p, lb_logits, hgrn_norm, s0, jnp.asarray(lhs, BF16), jnp.asarray(valid), jnp.asarray(cls), *cast)
    return outs[0], outs[1], outs[2:]


def _aligned_blocks(first, count):
    if first % count == 0:
        return [(first // count, count)]
    for head in range(1, count):
        if first % head == 0 and (first + head) % (count - head) == 0:
            return [(first // head, head), ((first + head) // (count - head), count - head)]
    raise ValueError(f"groups [{first}, {first + count}) cannot be covered by two aligned blocks")


def _merge_kernel(*refs, tiles_per_seq, n_gate_blocks):
    bcv_ref, c0_ref, cw_ref, yh_ref = refs[:4]
    gate_refs = refs[4:4 + n_gate_blocks]
    x_ref, wbc_ref, wbh_ref, wo_ref, o_ref, tail_ref, ubuf = refs[4 + n_gate_blocks:]

    @pl.when(pl.program_id(0) % tiles_per_seq == 0)
    def _():
        tail_ref[...] = c0_ref[...]

    seqs = tail_ref.shape[0]
    tm, d = x_ref.shape
    t = tm // seqs
    ng = bcv_ref.shape[0] // 3

    def wide(first, rs):
        return jnp.concatenate([bcv_ref[first + g, rs, :] for g in range(ng)], axis=1).astype(F32)

    def branch(groups, w_ref):
        acc = _dot(groups[0], w_ref[0])
        for g in range(1, len(groups)):
            acc = acc + _dot(groups[g], w_ref[g])
        return acc

    bh = branch([yh_ref[g] for g in range(yh_ref.shape[0])], wbh_ref)
    cw = cw_ref[...]
    ys = []
    for s in range(seqs):
        rs = slice(s * t, (s + 1) * t)
        y, tail = _conv_rows(wide(0, rs), wide(ng, rs), wide(2 * ng, rs), tail_ref[s], cw, ubuf)
        tail_ref[s] = tail
        ys.append(y.astype(BF16))
    yc = jnp.concatenate(ys, axis=0)
    bc = branch([yc[:, g * LANE_GROUP:(g + 1) * LANE_GROUP] for g in range(ng)], wbc_ref)
    gates = jax.nn.sigmoid(
        jnp.concatenate([r[g] for r in gate_refs for g in range(r.shape[0])], axis=1).astype(F32))
    merged = (gates[:, :d] * bc + gates[:, d:] * bh).astype(BF16)
    o_ref[...] = x_ref[...] + _dot(merged, wo_ref[...])


def _merge(p, yh, x, conv0, conv_w, wbc, wbh, wo, *, conv_group, gate_group, batch, tm):
    m, d = x.shape
    gw = p.shape[-1]
    dc = conv0.shape[-1]
    (conv_block, n_conv), = _aligned_blocks(conv_group, 3 * dc // gw)
    gate_blocks = _aligned_blocks(gate_group, 2 * d // gw)
    t = m // batch
    seqs_per_tile, tiles_per_seq = max(1, tm // t), max(1, t // tm)
    const = lambda shape: pl.BlockSpec(shape, lambda i: (0,) * len(shape), pipeline_mode=pl.Buffered(1))
    groups = lambda block, size: pl.BlockSpec((size, tm, gw), lambda i: (block, i, 0))
    tail_spec = pl.BlockSpec((seqs_per_tile, CONV_WIDTH - 1, dc), lambda i: (i // tiles_per_seq, 0, 0))
    return pl.pallas_call(
        functools.partial(_merge_kernel, tiles_per_seq=tiles_per_seq, n_gate_blocks=len(gate_blocks)),
        grid=(m // tm,),
        in_specs=[
            groups(conv_block, n_conv), tail_spec,
            pl.BlockSpec((CONV_WIDTH, dc), lambda i: (0, 0)),
            pl.BlockSpec((yh.shape[0], tm, gw), lambda i: (0, i, 0)),
            *[groups(block, size) for block, size in gate_blocks],
            pl.BlockSpec((tm, d), lambda i: (i, 0)),
            const(wbc.shape), const(wbh.shape), const(wo.shape),
        ],
        out_specs=[pl.BlockSpec((tm, d), lambda i: (i, 0)), tail_spec],
        out_shape=[jax.ShapeDtypeStruct((m, d), F32), jax.ShapeDtypeStruct(conv0.shape, F32)],
        scratch_shapes=[pltpu.VMEM((tm // seqs_per_tile + F32_SUBLANES, dc), F32)],
        compiler_params=_params(("arbitrary",)),
        name="conv_merge_out",
    )(p, conv0, conv_w, yh, *[p] * len(gate_blocks), x, wbc, wbh, wo)


MAX_ROW_TILE = 1024
FFN_CHUNK = 512
PROJ_CHUNK = 2560
MERGE_ROW_TILE = 256
HGRN_GROUP = 128
HGRN_CHUNK = 4096


def _plan_tiles(batch, t):
    m = batch * t
    tiles = dict(tm_ffn=min(MAX_ROW_TILE, m), tf=FFN_CHUNK, tm_proj=min(MAX_ROW_TILE, m), tn_proj=PROJ_CHUNK,
                 tm_merge=min(MERGE_ROW_TILE, m))
    if t >= HGRN_GROUP:
        tiles.update(seqs_hgrn=1, tc_hgrn=min(HGRN_CHUNK, t), group=HGRN_GROUP)
    else:
        tiles.update(seqs_hgrn=batch, tc_hgrn=t, group=t)
    return tiles


def _mix_and_ffn2(x1, conv0, s0, w, later, *, batch, tm_ffn, tf, tm_proj, tn_proj, tm_merge, seqs_hgrn, tc_hgrn, group):
    d = x1.shape[1]
    groups = w["in_groups"]
    p = _proj(x1, w["norm_mix"], w["in"], group_width=LANE_GROUP, tm=tm_proj, tn=tn_proj)
    pending = [a for a in later if a.dtype != BF16]
    yh, s_new, done = _hgrn(p, w["lb_logits"], w["hgrn_norm"], s0, first_group=groups["hgrn"], layer=0, batch=batch,
                            seqs=seqs_hgrn, tc=tc_hgrn, group=group, cast=pending)
    later = tuple(done) if pending else later
    wbc, wbh, wo, w2g, w2u, w2d = later
    grouped = lambda a: a.reshape(a.shape[0] // LANE_GROUP, LANE_GROUP, d)
    x2, conv_new = _merge(p, yh, x1, conv0, w["conv_w"], grouped(wbc), grouped(wbh), wo, conv_group=groups["conv"],
                          gate_group=groups["gate"], batch=batch, tm=tm_merge)
    y, _ = _ffn(x2, w["norm_ffn2"], w2g, w2u, w2d, w["norm_final"], final_norm=True, tm=tm_ffn, tf=tf)
    return y, conv_new, s_new, later


def _forward(x_prompt, x_sample, cache_conv, state_hgrn, norm_ffn1, w_ffn1_gate, w_ffn1_up, w_ffn1_down, norm_mix, w_in,
             conv_w, hgrn_lb_logits, hgrn_norm, w_br_conv, w_br_hgrn, w_out, norm_ffn2, w_ffn2_gate, w_ffn2_up,
             w_ffn2_down, norm_final, *, prompt_tiles=None, sample_tiles=None):
    assert w_in.shape[0] == 1, "single-layer trunk"
    prompt_tiles = prompt_tiles or _plan_tiles(*x_prompt.shape[:2])
    sample_tiles = sample_tiles or _plan_tiles(*x_sample.shape[:2])
    heads, dk, dv = state_hgrn.shape[-3:]
    assert dk == HEAD_DIM and dv == HEAD_DIM and heads % HEADS_PER_STEP == 0
    d = x_prompt.shape[-1]
    dc, dh = cache_conv.shape[-1], heads * dk
    bf = lambda a: a.astype(BF16)
    gf = norm_final.reshape(1, d)
    ffn1 = (norm_ffn1, bf(w_ffn1_gate[0]), bf(w_ffn1_up[0]), bf(w_ffn1_down[0]), gf)

    bp, tp, _ = x_prompt.shape
    x1p, (win,) = _ffn(x_prompt.reshape(bp * tp, d), *ffn1, final_norm=False, tm=prompt_tiles["tm_ffn"],
                       tf=prompt_tiles["tf"], cast=(w_in,))
    n_conv = 3 * dc
    w = dict(
        norm_mix=norm_mix, **{"in": win},
        in_groups=dict(conv=0, hgrn=n_conv // LANE_GROUP, gate=(n_conv + 4 * dh) // LANE_GROUP),
        conv_w=conv_w[0], lb_logits=hgrn_lb_logits, hgrn_norm=hgrn_norm, norm_ffn2=norm_ffn2, norm_final=gf,
    )
    later = (w_br_conv, w_br_hgrn, w_out, w_ffn2_gate, w_ffn2_up, w_ffn2_down)
    zero_conv = jnp.zeros((bp,) + cache_conv.shape[2:], F32)
    zero_hgrn = jnp.zeros((bp,) + state_hgrn.shape[2:], F32)
    yp, cp, sp, later = _mix_and_ffn2(x1p, zero_conv, zero_hgrn, w, later, batch=bp, **prompt_tiles)

    bs, ts, _ = x_sample.shape
    x1s, _ = _ffn(x_sample.reshape(bs * ts, d), *ffn1, final_norm=False, tm=sample_tiles["tm_ffn"], tf=sample_tiles["tf"])
    ys, cs, ss, _ = _mix_and_ffn2(x1s, cache_conv[0], state_hgrn[0], w, later, batch=bs, **sample_tiles)
    return yp.reshape(bp, tp, d), ys.reshape(bs, ts, d), cp[None], sp[None], cs[None], ss[None]


def kernel(x_prompt, x_sample, cache_conv, state_hgrn, norm_ffn1, w_ffn1_gate, w_ffn1_up, w_ffn1_down, norm_mix, w_in, conv_w, hgrn_lb_logits, hgrn_norm, w_br_conv, w_br_hgrn, w_out, norm_ffn2, w_ffn2_gate, w_ffn2_up, w_ffn2_down, norm_final):
    return _forward(
        x_prompt, x_sample, cache_conv, state_hgrn, norm_ffn1, w_ffn1_gate, w_ffn1_up, w_ffn1_down, norm_mix, w_in,
        conv_w, hgrn_lb_logits, hgrn_norm, w_br_conv, w_br_hgrn, w_out, norm_ffn2, w_ffn2_gate, w_ffn2_up, w_ffn2_down,
        norm_final)
```
